```python
import jax, jax.numpy as jnp
from jax import lax
import numpy as np

D_MODEL = 1024
BATCH = 8
SEQ = 8192
DEPTH = 1
DEC_BATCH = 2
DEC_SEQ = 16384
PAST_LEN = 128

GRID_W = 64
NA_HEADS = 8
NA_HEAD_DIM = 64
NA_WIN_H = 8
NA_WIN_W = 16
MLA_HEADS = 8
MLA_NOPE = 64
MLA_ROPE = 32
MLA_V = 64
Q_LORA = 384
KV_LORA = 256
ROPE_THETA = 10000.0
Q_BLOCK = 128
D_FF = -(-8 * D_MODEL // (3 * 256)) * 256
PLE_DIM = 256
EPS = 1e-6
NA_WIDTH = NA_HEADS * NA_HEAD_DIM
MLA_WIDTH = MLA_HEADS * MLA_V
IN_SPLITS = (NA_WIDTH, NA_WIDTH, NA_WIDTH, Q_LORA, KV_LORA, MLA_ROPE, 2 * D_MODEL)
IN_COLS = sum(IN_SPLITS)

kernel_name = 'hybrid_natten_mla_encoder'


def _rmsnorm(x, g):
    xf = x.astype(jnp.float32)
    y = xf * lax.rsqrt(jnp.mean(xf * xf, axis=-1, keepdims=True) + EPS)
    return (y * g.astype(jnp.float32)).astype(x.dtype)


def _rope(x, pos):
    half = x.shape[-1] // 2
    freqs = ROPE_THETA ** (-jnp.arange(half, dtype=jnp.float32) / half)
    ang = pos[:, None] * freqs[None, :]
    cos = jnp.cos(ang)[:, None, :]
    sin = jnp.sin(ang)[:, None, :]
    xf = x.astype(jnp.float32)
    x1, x2 = xf[..., :half], xf[..., half:]
    out = jnp.concatenate([x1 * cos - x2 * sin, x2 * cos + x1 * sin], axis=-1)
    return out.astype(x.dtype)


def _neighbourhood_attention(q, k, v, rpb):
    B, S, H, d = q.shape
    rows = S // GRID_W
    kh = min(NA_WIN_H, rows)
    kw = NA_WIN_W
    qg = (q * (d ** -0.5)).reshape(B, rows, GRID_W, H, d).transpose(1, 0, 2, 3, 4)
    kg = k.reshape(B, rows, GRID_W, H, d)
    vg = v.reshape(B, rows, GRID_W, H, d)
    cols = np.arange(GRID_W)
    cs = np.clip(cols - kw // 2, 0, GRID_W - kw)
    col_idx = cs[:, None] + np.arange(kw)[None, :]
    cb_idx = col_idx - cols[:, None] + (kw - 1)
    rpb_c = rpb[:, :, cb_idx]

    def row_fn(args):
        r, q_row = args
        rs = jnp.clip(r - kh // 2, 0, rows - kh)
        k_rows = lax.dynamic_slice_in_dim(kg, rs, kh, axis=1)
        v_rows = lax.dynamic_slice_in_dim(vg, rs, kh, axis=1)
        k_win = k_rows[:, :, col_idx]
        v_win = v_rows[:, :, col_idx]
        rb_idx = rs + jnp.arange(kh) - r + (NA_WIN_H - 1)
        bias = rpb_c[:, rb_idx].transpose(0, 2, 1, 3)
        s = jnp.einsum('bchd,bicjhd->bhcij', q_row, k_win).astype(jnp.float32)
        s = s + bias.astype(jnp.float32)[None]
        p = jax.nn.softmax(s.reshape(B, H, GRID_W, kh * kw), axis=-1).reshape(B, H, GRID_W, kh, kw)
        return jnp.einsum('bhcij,bicjhd->bchd', p.astype(v.dtype), v_win)

    out = lax.map(row_fn, (jnp.arange(rows), qg))
    return out.transpose(1, 0, 2, 3, 4).reshape(B, S, H * d)


def _latent_attention(zq, zkv, zr, g_q_lat, w_uq, g_kv_lat, w_ukv):
    B, S, _ = zq.shape
    pos = jnp.arange(S, dtype=jnp.float32)
    q = (_rmsnorm(zq, g_q_lat) @ w_uq).reshape(B, S, MLA_HEADS, MLA_NOPE + MLA_ROPE)
    kv = (_rmsnorm(zkv, g_kv_lat) @ w_ukv).reshape(B, S, MLA_HEADS, MLA_NOPE + MLA_V)
    k_nope, v = kv[..., :MLA_NOPE], kv[..., MLA_NOPE:]
    k_rope = _rope(zr[:, :, None, :], pos)
    scale = (MLA_NOPE + MLA_ROPE) ** -0.5
    q = jnp.concatenate([q[..., :MLA_NOPE], _rope(q[..., MLA_NOPE:], pos)], axis=-1) * scale
    k = jnp.concatenate([k_nope, jnp.broadcast_to(k_rope, (B, S, MLA_HEADS, MLA_ROPE))], axis=-1)
    nblk = S // Q_BLOCK
    qb = q.reshape(B, nblk, Q_BLOCK, MLA_HEADS, MLA_NOPE + MLA_ROPE).transpose(1, 0, 2, 3, 4)

    def blk(q_blk):
        s = jnp.einsum('bqhd,bkhd->bhqk', q_blk, k).astype(jnp.float32)
        p = jax.nn.softmax(s, axis=-1)
        return jnp.einsum('bhqk,bkhd->bqhd', p.astype(v.dtype), v)

    out = lax.map(blk, qb)
    return out.transpose(1, 0, 2, 3, 4).reshape(B, S, MLA_WIDTH)


def _layer(x, p_i, g_mix, w_in, rpb, g_q_lat, w_uq, g_kv_lat, w_ukv, w_na_o, w_mla_o, w_out,
           g_ffn, w_gate_up, w_down, g_ple, w_ple_gate, w_ple):
    B, S, _ = x.shape
    h = _rmsnorm(x, g_mix)
    z = h @ w_in
    zq_na, zk_na, zv_na, zq_lat, zkv_lat, zr, zg = jnp.split(z, list(np.cumsum(IN_SPLITS)[:-1]), axis=-1)
    hs = (B, S, NA_HEADS, NA_HEAD_DIM)
    na = _neighbourhood_attention(zq_na.reshape(hs), zk_na.reshape(hs), zv_na.reshape(hs), rpb)
    mla = _latent_attention(zq_lat, zkv_lat, zr, g_q_lat, w_uq, g_kv_lat, w_ukv)
    gates = jax.nn.sigmoid(zg.astype(jnp.float32)).astype(x.dtype)
    g_a, g_b = gates[..., :D_MODEL], gates[..., D_MODEL:]
    x = x + (g_a * (na @ w_na_o) + g_b * (mla @ w_mla_o)) @ w_out
    gu = _rmsnorm(x, g_ffn) @ w_gate_up
    x = x + (jax.nn.silu(gu[..., :D_FF]) * gu[..., D_FF:]) @ w_down
    ple_gate = jax.nn.sigmoid((_rmsnorm(x, g_ple) @ w_ple_gate).astype(jnp.float32)).astype(x.dtype)
    return x + ple_gate * (p_i @ w_ple)


def _trunk(x, p, g_mix, w_in, rpb, g_q_lat, w_uq, g_kv_lat, w_ukv, w_na_o, w_mla_o, w_out,
           g_ffn, w_gate_up, w_down, g_ple, w_ple_gate, w_ple, g_final):
    for i in range(DEPTH):
        x = _layer(x, p[i], g_mix[i], w_in[i], rpb[i], g_q_lat[i], w_uq[i], g_kv_lat[i], w_ukv[i],
                   w_na_o[i], w_mla_o[i], w_out[i], g_ffn[i], w_gate_up[i], w_down[i],
                   g_ple[i], w_ple_gate[i], w_ple[i])
    return _rmsnorm(x, g_final)


def setup_inputs(seed: int = 0) -> dict:
    key = jax.random.key(seed)
    ks = jax.random.split(key, 24)
    f32 = jnp.float32

    def w(k, shape, fan_in):
        return jax.random.normal(k, shape, f32) * (fan_in ** -0.5)

    def gain(k, shape):
        return 1.0 + 0.05 * jax.random.normal(k, shape, f32)

    L = DEPTH
    return {
        'x_prompt': jax.random.normal(ks[0], (BATCH, SEQ, D_MODEL), f32),
        'x_sample': jax.random.normal(ks[1], (DEC_BATCH, DEC_SEQ, D_MODEL), f32),
        'p_prompt': jax.random.normal(ks[2], (DEPTH, BATCH, SEQ, PLE_DIM), f32),
        'p_sample': jax.random.normal(ks[3], (DEPTH, DEC_BATCH, DEC_SEQ, PLE_DIM), f32),
        'g_mix': gain(ks[4], (L, D_MODEL)),
        'w_in': w(ks[5], (L, D_MODEL, IN_COLS), D_MODEL),
        'rpb': 0.1 * jax.random.normal(ks[6], (L, NA_HEADS, 2 * NA_WIN_H - 1, 2 * NA_WIN_W - 1), f32),
        'g_q_lat': gain(ks[7], (L, Q_LORA)),
        'w_uq': w(ks[8], (L, Q_LORA, MLA_HEADS * (MLA_NOPE + MLA_ROPE)), Q_LORA),
        'g_kv_lat': gain(ks[9], (L, KV_LORA)),
        'w_ukv': w(ks[10], (L, KV_LORA, MLA_HEADS * (MLA_NOPE + MLA_V)), KV_LORA),
        'w_na_o': w(ks[11], (L, NA_WIDTH, D_MODEL), NA_WIDTH),
        'w_mla_o': w(ks[12], (L, MLA_WIDTH, D_MODEL), MLA_WIDTH),
        'w_out': w(ks[13], (L, D_MODEL, D_MODEL), D_MODEL),
        'g_ffn': gain(ks[14], (L, D_MODEL)),
        'w_gate_up': w(ks[15], (L, D_MODEL, 2 * D_FF), D_MODEL),
        'w_down': w(ks[16], (L, D_FF, D_MODEL), D_FF),
        'g_ple': gain(ks[17], (L, D_MODEL)),
        'w_ple_gate': w(ks[18], (L, D_MODEL, D_MODEL), D_MODEL),
        'w_ple': w(ks[19], (L, PLE_DIM, D_MODEL), PLE_DIM),
        'g_final': gain(ks[20], (D_MODEL,)),
    }


def reference(x_prompt, x_sample, p_prompt, p_sample, g_mix, w_in, rpb, g_q_lat, w_uq, g_kv_lat,
              w_ukv, w_na_o, w_mla_o, w_out, g_ffn, w_gate_up, w_down, g_ple, w_ple_gate, w_ple, g_final):
    y_prompt = _trunk(x_prompt, p_prompt, g_mix, w_in, rpb, g_q_lat, w_uq, g_kv_lat, w_ukv, w_na_o,
                      w_mla_o, w_out, g_ffn, w_gate_up, w_down, g_ple, w_ple_gate, w_ple, g_final)
    y_sample = _trunk(x_sample, p_sample, g_mix, w_in, rpb, g_q_lat, w_uq, g_kv_lat, w_ukv, w_na_o,
                      w_mla_o, w_out, g_ffn, w_gate_up, w_down, g_ple, w_ple_gate, w_ple, g_final)
    return (y_prompt, y_sample)
```

```python
import functools

import numpy as np
import jax
import jax.numpy as jnp
from jax import lax
from jax.experimental import pallas as pl
from jax.experimental.pallas import tpu as pltpu

D_MODEL = 1024
GRID_W = 64
NA_HEADS = 8
NA_HEAD_DIM = 64
NA_WIN_H = 8
NA_WIN_W = 16
MLA_HEADS = 8
MLA_NOPE = 64
MLA_ROPE = 32
MLA_V = 64
Q_LORA = 384
KV_LORA = 256
ROPE_THETA = 10000.0
D_FF = 2816
PLE_DIM = 256
EPS = 1e-6
NA_WIDTH = NA_HEADS * NA_HEAD_DIM
MLA_WIDTH = MLA_HEADS * MLA_V

MLA_QK_PAD = 128
NA_GROUP_ROWS = 8
NA_KEY_ROWS = 16
NA_V_CHUNK = 256
MASK_VALUE = -1e30

TM_IN = 256
TM_POST = 256
MLA_TQ = 512
MLA_TK = 512
FF_CHUNK = 256
VMEM_LIMIT_BYTES = 56 * 1024 * 1024

BF16 = jnp.bfloat16
F32 = jnp.float32


def _dot(a, b):
    return jnp.dot(a, b, preferred_element_type=F32)


def _dot_nt(a, b):
    return lax.dot_general(a, b, (((1,), (1,)), ((), ())), preferred_element_type=F32)


def _dot_tn(a, b):
    return lax.dot_general(a, b, (((0,), (0,)), ((), ())), preferred_element_type=F32)


def _rms(x, g):
    ms = jnp.mean(x * x, axis=-1, keepdims=True)
    return x * lax.rsqrt(ms + EPS) * g


def _const_spec(shape):
    nd = len(shape)
    return pl.BlockSpec(shape, lambda *_: (0,) * nd, pipeline_mode=pl.Buffered(1))


def _inproj_body(x_ref, gmix_ref, wkna_ref, wqnaT_ref, wvnaT_ref, wzq_ref, wzkv_ref, wzrr_ref, wg_ref,
                 gq_ref, wuqT_ref, gkv_ref, wuk_ref, wuvT_ref, place_ref,
                 cosT_ref, sinT_ref, cos_ref, sin_ref,
                 kna_ref, qTna_ref, vTna_ref, gates_ref, qT_ref, k_ref, vT_ref):
    tm = x_ref.shape[1]
    h = _rms(x_ref[0], gmix_ref[...]).astype(BF16)

    kna_ref[0] = _dot(h, wkna_ref[...]).astype(BF16)
    qTna_ref[0] = _dot_nt(wqnaT_ref[...], h).astype(BF16)
    vTna = _dot_nt(wvnaT_ref[...], h).astype(BF16)
    for c in range(tm // NA_V_CHUNK):
        vTna_ref[0, c] = vTna[:, c * NA_V_CHUNK:(c + 1) * NA_V_CHUNK]

    gates_ref[0] = jax.nn.sigmoid(_dot(h, wg_ref[...]))

    zqn = _rms(_dot(h, wzq_ref[...]), gq_ref[...]).astype(BF16)
    zkvn = _rms(_dot(h, wzkv_ref[...]), gkv_ref[...]).astype(BF16)

    scale = (MLA_NOPE + MLA_ROPE) ** -0.5
    qT = _dot_nt(wuqT_ref[...], zqn)
    cT = cosT_ref[...]
    sT = sinT_ref[...]
    half = MLA_ROPE // 2
    for hh in range(MLA_HEADS):
        base = hh * MLA_QK_PAD
        nope = qT[base:base + MLA_NOPE]
        x1 = qT[base + MLA_NOPE:base + MLA_NOPE + half]
        x2 = qT[base + MLA_NOPE + half:base + MLA_NOPE + MLA_ROPE]
        qT_ref[0, base:base + MLA_NOPE, :] = (nope * scale).astype(BF16)
        qT_ref[0, base + MLA_NOPE:base + MLA_NOPE + half, :] = ((x1 * cT - x2 * sT) * scale).astype(BF16)
        qT_ref[0, base + MLA_NOPE + half:base + MLA_NOPE + MLA_ROPE, :] = ((x2 * cT + x1 * sT) * scale).astype(BF16)
        qT_ref[0, base + MLA_NOPE + MLA_ROPE:base + MLA_QK_PAD, :] = jnp.zeros(
            (MLA_QK_PAD - MLA_NOPE - MLA_ROPE, tm), BF16)

    zrr = _dot(h, wzrr_ref[...])
    k_rope = zrr[:, :MLA_ROPE] * cos_ref[...] + zrr[:, MLA_ROPE:] * sin_ref[...]
    k_nope = _dot(zkvn, wuk_ref[...])
    k_ref[0] = (k_nope + _dot(k_rope.astype(BF16), place_ref[...])).astype(BF16)

    vT_ref[0, 0] = _dot_nt(wuvT_ref[...], zkvn).astype(BF16)


def _inproj(x, w, tabs):
    B, S, D = x.shape
    tm = TM_IN
    nt = S // tm
    sub = MLA_TK // tm
    hq = MLA_HEADS * MLA_QK_PAD
    weights = (w["g_mix"], w["w_kna"], w["w_qnaT"], w["w_vnaT"], w["w_zq"], w["w_zkv"], w["w_zrr"], w["w_g"],
               w["g_q"], w["w_uqT"], w["g_kv"], w["w_uk"], w["w_uvT"], w["place"])
    in_specs = [pl.BlockSpec((1, tm, D), lambda b, i: (b, i, 0))]
    in_specs += [_const_spec(a.shape) for a in weights]
    in_specs += [
        pl.BlockSpec((MLA_ROPE // 2, tm), lambda b, i: (0, i)),
        pl.BlockSpec((MLA_ROPE // 2, tm), lambda b, i: (0, i)),
        pl.BlockSpec((tm, MLA_ROPE), lambda b, i: (i, 0)),
        pl.BlockSpec((tm, MLA_ROPE), lambda b, i: (i, 0)),
    ]
    out_shape = (
        jax.ShapeDtypeStruct((B, S, NA_WIDTH), BF16),
        jax.ShapeDtypeStruct((B, NA_WIDTH, S), BF16),
        jax.ShapeDtypeStruct((B, S // NA_V_CHUNK, NA_WIDTH, NA_V_CHUNK), BF16),
        jax.ShapeDtypeStruct((B, S, 2 * D), F32),
        jax.ShapeDtypeStruct((B, hq, S), BF16),
        jax.ShapeDtypeStruct((B, S, hq), BF16),
        jax.ShapeDtypeStruct((B, S // MLA_TK, MLA_WIDTH, MLA_TK), BF16),
    )
    out_specs = (
        pl.BlockSpec((1, tm, NA_WIDTH), lambda b, i: (b, i, 0)),
        pl.BlockSpec((1, NA_WIDTH, tm), lambda b, i: (b, 0, i)),
        pl.BlockSpec((1, tm // NA_V_CHUNK, NA_WIDTH, NA_V_CHUNK), lambda b, i: (b, i, 0, 0)),
        pl.BlockSpec((1, tm, 2 * D), lambda b, i: (b, i, 0)),
        pl.BlockSpec((1, hq, tm), lambda b, i: (b, 0, i)),
        pl.BlockSpec((1, tm, hq), lambda b, i: (b, i, 0)),
        pl.BlockSpec((1, 1, MLA_WIDTH, tm), lambda b, i: (b, i // sub, 0, i % sub)),
    )
    return pl.pallas_call(
        _inproj_body,
        grid=(B, nt),
        in_specs=in_specs,
        out_specs=out_specs,
        out_shape=out_shape,
        compiler_params=pltpu.CompilerParams(
            dimension_semantics=("arbitrary", "arbitrary"), vmem_limit_bytes=VMEM_LIMIT_BYTES),
        name="inproj",
    )(x, *weights, tabs["cosT"][:, :S], tabs["sinT"][:, :S], tabs["cos"][:S], tabs["sin"][:S])


def _na_body(qT_ref, k_ref, vT_ref, bias_ref, o_ref, *, rows):
    g = pl.program_id(2)
    nq = NA_GROUP_ROWS * GRID_W
    nk = NA_KEY_ROWS * GRID_W
    ws = jnp.clip(NA_GROUP_ROWS * g - NA_WIN_H // 2, 0, rows - NA_KEY_ROWS)
    tok0 = pl.multiple_of(ws * GRID_W, NA_V_CHUNK)
    c0 = lax.shift_right_logical(ws, 2)
    kwin = k_ref[0, pl.ds(tok0, nk), :]
    qT = qT_ref[0]
    row = lax.broadcasted_iota(jnp.int32, qT.shape, 0)
    for hh in range(2):
        lo = hh * NA_HEAD_DIM
        q_h = jnp.where((row >= lo) & (row < lo + NA_HEAD_DIM), qT, jnp.zeros_like(qT))
        s = _dot(kwin, q_h) + bias_ref[0, hh]
        m = jnp.max(s, axis=0, keepdims=True)
        p = jnp.exp(s - m)
        l = jnp.sum(p, axis=0, keepdims=True)
        pb = p.astype(BF16)
        acc = jnp.zeros((NA_HEAD_DIM, nq), F32)
        for c in range(nk // NA_V_CHUNK):
            v_c = vT_ref[0, c0 + c, lo:lo + NA_HEAD_DIM, :]
            acc = acc + _dot(v_c, pb[c * NA_V_CHUNK:(c + 1) * NA_V_CHUNK])
        o_ref[0, lo:lo + NA_HEAD_DIM, :] = (acc / l).astype(BF16)


def _na(qT, k, vT, biasT):
    B, _, S = qT.shape
    rows = S // GRID_W
    nq = NA_GROUP_ROWS * GRID_W
    nk = NA_KEY_ROWS * GRID_W
    ng = rows // NA_GROUP_ROWS
    pair = 2 * NA_HEAD_DIM

    def bias_idx(b, hp, g):
        kind = jnp.where(g == 0, 0, jnp.where(g == ng - 1, 2, 1))
        return (kind, hp, 0, 0)

    return pl.pallas_call(
        functools.partial(_na_body, rows=rows),
        grid=(B, NA_HEADS // 2, ng),
        in_specs=[
            pl.BlockSpec((1, pair, nq), lambda b, hp, g: (b, hp, g)),
            pl.BlockSpec((1, S, pair), lambda b, hp, g: (b, 0, hp)),
            pl.BlockSpec((1, S // NA_V_CHUNK, pair, NA_V_CHUNK), lambda b, hp, g: (b, 0, hp, 0)),
            pl.BlockSpec((1, 2, nk, nq), bias_idx),
        ],
        out_specs=pl.BlockSpec((1, pair, nq), lambda b, hp, g: (b, hp, g)),
        out_shape=jax.ShapeDtypeStruct((B, NA_WIDTH, S), BF16),
        compiler_params=pltpu.CompilerParams(
            dimension_semantics=("arbitrary", "arbitrary", "arbitrary"), vmem_limit_bytes=VMEM_LIMIT_BYTES),
        name="natten",
    )(qT, k, vT, biasT)


def _na_bias_table(rpb):
    rows = 4 * NA_KEY_ROWS
    kinds = ((0, 0), (NA_KEY_ROWS, NA_KEY_ROWS - NA_WIN_H // 2), (rows - NA_GROUP_ROWS, rows - NA_KEY_ROWS))
    ql = np.arange(NA_GROUP_ROWS * GRID_W)
    kl = np.arange(NA_KEY_ROWS * GRID_W)
    q_row_l, q_col = ql // GRID_W, ql % GRID_W
    k_row_l, k_col = kl // GRID_W, kl % GRID_W
    cs = np.clip(q_col - NA_WIN_W // 2, 0, GRID_W - NA_WIN_W)
    col_ok = (k_col[:, None] >= cs[None, :]) & (k_col[:, None] < cs[None, :] + NA_WIN_W)
    col_idx = np.clip(k_col[:, None] - q_col[None, :] + NA_WIN_W - 1, 0, 2 * NA_WIN_W - 2)
    tables = []
    for r0, ws in kinds:
        q_row = r0 + q_row_l
        k_row = ws + k_row_l
        rs = np.clip(q_row - NA_WIN_H // 2, 0, rows - NA_WIN_H)
        row_ok = (k_row[:, None] >= rs[None, :]) & (k_row[:, None] < rs[None, :] + NA_WIN_H)
        row_idx = np.clip(k_row[:, None] - q_row[None, :] + NA_WIN_H - 1, 0, 2 * NA_WIN_H - 2)
        vals = rpb[:, row_idx, col_idx]
        tables.append(jnp.where((row_ok & col_ok)[None], vals, MASK_VALUE))
    return jnp.stack(tables).astype(F32)


def _mla_body(qT_ref, k_ref, vT_ref, o_ref):
    tq = qT_ref.shape[2]
    tk = vT_ref.shape[3]
    nchunk = vT_ref.shape[1]
    qT = qT_ref[0]

    def step(j, carry):
        m, l, acc = carry
        k_c = k_ref[0, pl.ds(pl.multiple_of(j * tk, tk), tk), :]
        s = _dot(k_c, qT)
        m_new = jnp.maximum(m, jnp.max(s, axis=0, keepdims=True))
        alpha = jnp.exp(m - m_new)
        p = jnp.exp(s - m_new)
        l = alpha * l + jnp.sum(p, axis=0, keepdims=True)
        acc = alpha * acc + _dot(vT_ref[0, j], p.astype(BF16))
        return m_new, l, acc

    init = (jnp.full((1, tq), -jnp.inf, F32), jnp.zeros((1, tq), F32), jnp.zeros((MLA_V, tq), F32))
    _, l, acc = lax.fori_loop(0, nchunk, step, init)
    o_ref[0] = (acc / l).astype(BF16)


def _mla(qT, k, vT):
    B, _, S = qT.shape
    tq = MLA_TQ
    nchunk, tk = vT.shape[1], vT.shape[3]
    return pl.pallas_call(
        _mla_body,
        grid=(B, MLA_HEADS, S // tq),
        in_specs=[
            pl.BlockSpec((1, MLA_QK_PAD, tq), lambda b, h, i: (b, h, i)),
            pl.BlockSpec((1, S, MLA_QK_PAD), lambda b, h, i: (b, 0, h)),
            pl.BlockSpec((1, nchunk, MLA_V, tk), lambda b, h, i: (b, 0, h, 0)),
        ],
        out_specs=pl.BlockSpec((1, MLA_V, tq), lambda b, h, i: (b, h, i)),
        out_shape=jax.ShapeDtypeStruct((B, MLA_WIDTH, S), BF16),
        compiler_params=pltpu.CompilerParams(
            dimension_semantics=("arbitrary", "arbitrary", "arbitrary"), vmem_limit_bytes=VMEM_LIMIT_BYTES),
        name="mla",
    )(qT, k, vT)


def _post_body(x_ref, oTna_ref, oTmla_ref, gates_ref, p_ref,
               wnao_ref, wmlao_ref, wout_ref, gffn_ref, wgate_ref, wup_ref, wdown_ref,
               gple_ref, wpleg_ref, wple_ref, gfin_ref, y_ref, acc_ref):
    d = x_ref.shape[2]
    a = _dot_tn(oTna_ref[0], wnao_ref[...])
    b = _dot_tn(oTmla_ref[0], wmlao_ref[...])
    gates = gates_ref[0]
    mix = (gates[:, :d] * a + gates[:, d:] * b).astype(BF16)
    x1 = x_ref[0] + _dot(mix, wout_ref[...])

    hn = _rms(x1, gffn_ref[...]).astype(BF16)
    acc_ref[...] = jnp.zeros_like(acc_ref)

    def ff_step(c, carry):
        gate = _dot(hn, wgate_ref[c])
        up = _dot(hn, wup_ref[c])
        act = (gate * jax.nn.sigmoid(gate) * up).astype(BF16)
        acc_ref[...] += _dot(act, wdown_ref[c])
        return carry

    lax.fori_loop(0, wgate_ref.shape[0], ff_step, 0)
    x2 = x1 + acc_ref[...]

    ple_gate = jax.nn.sigmoid(_dot(_rms(x2, gple_ref[...]).astype(BF16), wpleg_ref[...]))
    x3 = x2 + ple_gate * _dot(p_ref[0].astype(BF16), wple_ref[...])
    y_ref[0] = _rms(x3, gfin_ref[...])


def _post(x, oTna, oTmla, gates, p, w):
    B, S, D = x.shape
    tm = TM_POST
    weights = (w["w_na_o"], w["w_mla_o"], w["w_out"], w["g_ffn"], w["w_gate"], w["w_up"], w["w_down"],
               w["g_ple"], w["w_ple_gate"], w["w_ple"], w["g_final"])
    in_specs = [
        pl.BlockSpec((1, tm, D), lambda b, i: (b, i, 0)),
        pl.BlockSpec((1, NA_WIDTH, tm), lambda b, i: (b, 0, i)),
        pl.BlockSpec((1, MLA_WIDTH, tm), lambda b, i: (b, 0, i)),
        pl.BlockSpec((1, tm, 2 * D), lambda b, i: (b, i, 0)),
        pl.BlockSpec((1, tm, PLE_DIM), lambda b, i: (b, i, 0)),
    ]
    in_specs += [_const_spec(a.shape) for a in weights]
    return pl.pallas_call(
        _post_body,
        grid=(B, S // tm),
        in_specs=in_specs,
        out_specs=pl.BlockSpec((1, tm, D), lambda b, i: (b, i, 0)),
        out_shape=jax.ShapeDtypeStruct((B, S, D), F32),
        scratch_shapes=[pltpu.VMEM((tm, D), F32)],
        compiler_params=pltpu.CompilerParams(
            dimension_semantics=("arbitrary", "arbitrary"), vmem_limit_bytes=VMEM_LIMIT_BYTES),
        name="post",
    )(x, oTna, oTmla, gates, p, *weights)


def _prep_weights(g_mix, w_in, g_q_lat, w_uq, g_kv_lat, w_ukv, w_na_o, w_mla_o, w_out,
                  g_ffn, w_gate_up, w_down, g_ple, w_ple_gate, w_ple, g_final):
    o = np.cumsum((0, NA_WIDTH, NA_WIDTH, NA_WIDTH, Q_LORA, KV_LORA, MLA_ROPE, 2 * D_MODEL))
    w_q, w_k, w_v, w_zq, w_zkv, w_zr, w_g = (w_in[:, o[i]:o[i + 1]] for i in range(7))
    half = MLA_ROPE // 2
    w_zr_rot = jnp.concatenate([-w_zr[:, half:], w_zr[:, :half]], axis=1)

    w_uq_h = w_uq.reshape(Q_LORA, MLA_HEADS, MLA_NOPE + MLA_ROPE)
    w_uq_pad = jnp.pad(w_uq_h, ((0, 0), (0, 0), (0, MLA_QK_PAD - MLA_NOPE - MLA_ROPE)))
    w_uqT = w_uq_pad.reshape(Q_LORA, MLA_HEADS * MLA_QK_PAD).T

    w_ukv_h = w_ukv.reshape(KV_LORA, MLA_HEADS, MLA_NOPE + MLA_V)
    w_uk = jnp.pad(w_ukv_h[:, :, :MLA_NOPE], ((0, 0), (0, 0), (0, MLA_QK_PAD - MLA_NOPE)))
    w_uk = w_uk.reshape(KV_LORA, MLA_HEADS * MLA_QK_PAD)
    w_uvT = w_ukv_h[:, :, MLA_NOPE:].reshape(KV_LORA, MLA_WIDTH).T

    place = np.zeros((MLA_ROPE, MLA_HEADS, MLA_QK_PAD), np.float32)
    for j in range(MLA_ROPE):
        place[j, :, MLA_NOPE + j] = 1.0
    place = place.reshape(MLA_ROPE, MLA_HEADS * MLA_QK_PAD)

    nc = D_FF // FF_CHUNK
    w_gate = w_gate_up[:, :D_FF].reshape(D_MODEL, nc, FF_CHUNK).transpose(1, 0, 2)
    w_up = w_gate_up[:, D_FF:].reshape(D_MODEL, nc, FF_CHUNK).transpose(1, 0, 2)
    w_dn = w_down.reshape(nc, FF_CHUNK, D_MODEL)

    row = lambda g: g.reshape(1, -1).astype(F32)
    return {
        "g_mix": row(g_mix),
        "w_kna": w_k.astype(BF16),
        "w_qnaT": (w_q * NA_HEAD_DIM ** -0.5).T.astype(BF16),
        "w_vnaT": w_v.T.astype(BF16),
        "w_zq": w_zq.astype(BF16),
        "w_zkv": w_zkv.astype(BF16),
        "w_zrr": jnp.concatenate([w_zr, w_zr_rot], axis=1).astype(BF16),
        "w_g": w_g.astype(BF16),
        "g_q": row(g_q_lat),
        "w_uqT": w_uqT.astype(BF16),
        "g_kv": row(g_kv_lat),
        "w_uk": w_uk.astype(BF16),
        "w_uvT": w_uvT.astype(BF16),
        "place": jnp.asarray(place, BF16),
        "w_na_o": w_na_o.astype(BF16),
        "w_mla_o": w_mla_o.astype(BF16),
        "w_out": w_out.astype(BF16),
        "g_ffn": row(g_ffn),
        "w_gate": w_gate.astype(BF16),
        "w_up": w_up.astype(BF16),
        "w_down": w_dn.astype(BF16),
        "g_ple": row(g_ple),
        "w_ple_gate": w_ple_gate.astype(BF16),
        "w_ple": w_ple.astype(BF16),
        "g_final": row(g_final),
    }


def _rope_tables(s_max):
    half = MLA_ROPE // 2
    pos = jnp.arange(s_max, dtype=F32)
    freqs = ROPE_THETA ** (-jnp.arange(half, dtype=F32) / half)
    ang = pos[:, None] * freqs[None, :]
    cos, sin = jnp.cos(ang), jnp.sin(ang)
    return {"cosT": cos.T, "sinT": sin.T,
            "cos": jnp.concatenate([cos, cos], axis=1), "sin": jnp.concatenate([sin, sin], axis=1)}


def _trunk(x, p, w, tabs, biasT):
    kna, qTna, vTna, gates, qT, k, vT = _inproj(x, w, tabs)
    oTna = _na(qTna, kna, vTna, biasT)
    oTmla = _mla(qT, k, vT)
    return _post(x, oTna, oTmla, gates, p, w)


def kernel(x_prompt, x_sample, p_prompt, p_sample, g_mix, w_in, rpb, g_q_lat, w_uq, g_kv_lat, w_ukv,
           w_na_o, w_mla_o, w_out, g_ffn, w_gate_up, w_down, g_ple, w_ple_gate, w_ple, g_final):
    assert g_mix.shape[0] == 1, "single-layer trunk"
    w = _prep_weights(g_mix[0], w_in[0], g_q_lat[0], w_uq[0], g_kv_lat[0], w_ukv[0], w_na_o[0], w_mla_o[0],
                      w_out[0], g_ffn[0], w_gate_up[0], w_down[0], g_ple[0], w_ple_gate[0], w_ple[0], g_final)
    biasT = _na_bias_table(rpb[0])
    tabs = _rope_tables(max(x_prompt.shape[1], x_sample.shape[1]))
    y_prompt = _trunk(x_prompt, p_prompt[0], w, tabs, biasT)
    y_sample = _trunk(x_sample, p_sample[0], w, tabs, biasT)
    return (y_prompt, y_sample)
```

```python
import functools

import numpy as np
import jax
import jax.numpy as jnp
from jax import lax
from jax.experimental import pallas as pl
from jax.experimental.pallas import tpu as pltpu

D_MODEL = 1024
GRID_W = 64
NA_HEADS = 8
NA_HEAD_DIM = 64
NA_WIN_H = 8
NA_WIN_W = 16
MLA_HEADS = 8
MLA_NOPE = 64
MLA_ROPE = 32
MLA_V = 64
Q_LORA = 384
KV_LORA = 256
ROPE_THETA = 10000.0
D_FF = 2816
PLE_DIM = 256
EPS = 1e-6
NA_WIDTH = NA_HEADS * NA_HEAD_DIM
MLA_WIDTH = MLA_HEADS * MLA_V

MLA_QK_PAD = 128
MLA_V_PAD = 80
LOG2_E = 1.4426950408889634
NA_GROUP_ROWS = 8
NA_KEY_ROWS = 16
NA_V_CHUNK = 256
MASK_VALUE = -1e30

TM_IN = 256
TM_POST = 256
MLA_TQ = 512
MLA_TK = 512
MLA_UNROLL = 8
VMEM_LIMIT_BYTES = 56 * 1024 * 1024

BF16 = jnp.bfloat16
F32 = jnp.float32


def _dot(a, b):
    return jnp.dot(a, b, preferred_element_type=F32)


def _dot_nt(a, b):
    return lax.dot_general(a, b, (((1,), (1,)), ((), ())), preferred_element_type=F32)


def _dot_tn(a, b):
    return lax.dot_general(a, b, (((0,), (0,)), ((), ())), preferred_element_type=F32)


def _rms(x, g):
    ms = jnp.mean(x * x, axis=-1, keepdims=True)
    return x * lax.rsqrt(ms + EPS) * g


def _const_spec(shape):
    nd = len(shape)
    return pl.BlockSpec(shape, lambda *_: (0,) * nd, pipeline_mode=pl.Buffered(1))


def _inproj_body(x_ref, gmix_ref, wkna_ref, wqnaT_ref, wvnaT_ref, wzq_ref, wzkv_ref, wzrr_ref, wg_ref,
                 gq_ref, wuqT_ref, gkv_ref, wuk_ref, wuvT_ref, place_ref,
                 cosT_ref, sinT_ref, cos_ref, sin_ref,
                 kna_ref, qTna_ref, vTna_ref, gates_ref, qT_ref, k_ref, vT_ref):
    tm = x_ref.shape[1]
    h = _rms(x_ref[0], gmix_ref[...]).astype(BF16)

    kna_ref[0] = _dot(h, wkna_ref[...]).astype(BF16)
    qTna_ref[0] = _dot_nt(wqnaT_ref[...], h).astype(BF16)
    vTna = _dot_nt(wvnaT_ref[...], h).astype(BF16)
    for c in range(tm // NA_V_CHUNK):
        vTna_ref[0, c] = vTna[:, c * NA_V_CHUNK:(c + 1) * NA_V_CHUNK]

    gates_ref[0] = jax.nn.sigmoid(_dot(h, wg_ref[...]))

    zqn = _rms(_dot(h, wzq_ref[...]), gq_ref[...]).astype(BF16)
    zkvn = _rms(_dot(h, wzkv_ref[...]), gkv_ref[...]).astype(BF16)

    scale = (MLA_NOPE + MLA_ROPE) ** -0.5 * LOG2_E
    qT = _dot_nt(wuqT_ref[...], zqn)
    cT = cosT_ref[...]
    sT = sinT_ref[...]
    half = MLA_ROPE // 2
    for hh in range(MLA_HEADS):
        base = hh * MLA_QK_PAD
        nope = qT[base:base + MLA_NOPE]
        x1 = qT[base + MLA_NOPE:base + MLA_NOPE + half]
        x2 = qT[base + MLA_NOPE + half:base + MLA_NOPE + MLA_ROPE]
        qT_ref[0, base:base + MLA_NOPE, :] = (nope * scale).astype(BF16)
        qT_ref[0, base + MLA_NOPE:base + MLA_NOPE + half, :] = ((x1 * cT - x2 * sT) * scale).astype(BF16)
        qT_ref[0, base + MLA_NOPE + half:base + MLA_NOPE + MLA_ROPE, :] = ((x2 * cT + x1 * sT) * scale).astype(BF16)
        qT_ref[0, base + MLA_NOPE + MLA_ROPE:base + MLA_QK_PAD, :] = jnp.zeros(
            (MLA_QK_PAD - MLA_NOPE - MLA_ROPE, tm), BF16)

    zrr = _dot(h, wzrr_ref[...])
    k_rope = zrr[:, :MLA_ROPE] * cos_ref[...] + zrr[:, MLA_ROPE:] * sin_ref[...]
    k_nope = _dot(zkvn, wuk_ref[...])
    k_ref[0] = (k_nope + _dot(k_rope.astype(BF16), place_ref[...])).astype(BF16)

    vT = _dot_nt(wuvT_ref[...], zkvn).astype(BF16)
    pad_rows = MLA_V_PAD - MLA_V
    ones_row = (lax.broadcasted_iota(jnp.int32, (pad_rows, tm), 0) == 0).astype(F32).astype(BF16)
    for hh in range(MLA_HEADS):
        vT_ref[0, 0, hh * MLA_V_PAD:hh * MLA_V_PAD + MLA_V, :] = vT[hh * MLA_V:(hh + 1) * MLA_V]
        vT_ref[0, 0, hh * MLA_V_PAD + MLA_V:(hh + 1) * MLA_V_PAD, :] = ones_row


def _inproj(x, w, tabs):
    B, S, D = x.shape
    tm = TM_IN
    nt = S // tm
    sub = MLA_TK // tm
    hq = MLA_HEADS * MLA_QK_PAD
    weights = (w["g_mix"], w["w_kna"], w["w_qnaT"], w["w_vnaT"], w["w_zq"], w["w_zkv"], w["w_zrr"], w["w_g"],
               w["g_q"], w["w_uqT"], w["g_kv"], w["w_uk"], w["w_uvT"], w["place"])
    in_specs = [pl.BlockSpec((1, tm, D), lambda b, i: (b, i, 0))]
    in_specs += [_const_spec(a.shape) for a in weights]
    in_specs += [
        pl.BlockSpec((MLA_ROPE // 2, tm), lambda b, i: (0, i)),
        pl.BlockSpec((MLA_ROPE // 2, tm), lambda b, i: (0, i)),
        pl.BlockSpec((tm, MLA_ROPE), lambda b, i: (i, 0)),
        pl.BlockSpec((tm, MLA_ROPE), lambda b, i: (i, 0)),
    ]
    out_shape = (
        jax.ShapeDtypeStruct((B, S, NA_WIDTH), BF16),
        jax.ShapeDtypeStruct((B, NA_WIDTH, S), BF16),
        jax.ShapeDtypeStruct((B, S // NA_V_CHUNK, NA_WIDTH, NA_V_CHUNK), BF16),
        jax.ShapeDtypeStruct((B, S, 2 * D), F32),
        jax.ShapeDtypeStruct((B, hq, S), BF16),
        jax.ShapeDtypeStruct((B, S, hq), BF16),
        jax.ShapeDtypeStruct((B, S // MLA_TK, MLA_HEADS * MLA_V_PAD, MLA_TK), BF16),
    )
    out_specs = (
        pl.BlockSpec((1, tm, NA_WIDTH), lambda b, i: (b, i, 0)),
        pl.BlockSpec((1, NA_WIDTH, tm), lambda b, i: (b, 0, i)),
        pl.BlockSpec((1, tm // NA_V_CHUNK, NA_WIDTH, NA_V_CHUNK), lambda b, i: (b, i, 0, 0)),
        pl.BlockSpec((1, tm, 2 * D), lambda b, i: (b, i, 0)),
        pl.BlockSpec((1, hq, tm), lambda b, i: (b, 0, i)),
        pl.BlockSpec((1, tm, hq), lambda b, i: (b, i, 0)),
        pl.BlockSpec((1, 1, MLA_HEADS * MLA_V_PAD, tm), lambda b, i: (b, i // sub, 0, i % sub)),
    )
    return pl.pallas_call(
        _inproj_body,
        grid=(B, nt),
        in_specs=in_specs,
        out_specs=out_specs,
        out_shape=out_shape,
        compiler_params=pltpu.CompilerParams(
            dimension_semantics=("arbitrary", "arbitrary"), vmem_limit_bytes=VMEM_LIMIT_BYTES),
        name="inproj",
    )(x, *weights, tabs["cosT"][:, :S], tabs["sinT"][:, :S], tabs["cos"][:S], tabs["sin"][:S])


def _na_body(qT_ref, k_ref, vT_ref, bias_ref, o_ref, *, rows):
    g = pl.program_id(2)
    nq = NA_GROUP_ROWS * GRID_W
    nk = NA_KEY_ROWS * GRID_W
    ws = jnp.clip(NA_GROUP_ROWS * g - NA_WIN_H // 2, 0, rows - NA_KEY_ROWS)
    tok0 = pl.multiple_of(ws * GRID_W, NA_V_CHUNK)
    c0 = lax.shift_right_logical(ws, 2)
    kwin = k_ref[0, pl.ds(tok0, nk), :]
    qT = qT_ref[0]
    row = lax.broadcasted_iota(jnp.int32, qT.shape, 0)
    for hh in range(2):
        lo = hh * NA_HEAD_DIM
        q_h = jnp.where((row >= lo) & (row < lo + NA_HEAD_DIM), qT, jnp.zeros_like(qT))
        s = _dot(kwin, q_h) + bias_ref[0, hh]
        m = jnp.max(s, axis=0, keepdims=True)
        p = jnp.exp(s - m)
        l = jnp.sum(p, axis=0, keepdims=True)
        pb = p.astype(BF16)
        acc = jnp.zeros((NA_HEAD_DIM, nq), F32)
        for c in range(nk // NA_V_CHUNK):
            v_c = vT_ref[0, c0 + c, lo:lo + NA_HEAD_DIM, :]
            acc = acc + _dot(v_c, pb[c * NA_V_CHUNK:(c + 1) * NA_V_CHUNK])
        o_ref[0, lo:lo + NA_HEAD_DIM, :] = (acc / l).astype(BF16)


def _na(qT, k, vT, biasT):
    B, _, S = qT.shape
    rows = S // GRID_W
    nq = NA_GROUP_ROWS * GRID_W
    nk = NA_KEY_ROWS * GRID_W
    ng = rows // NA_GROUP_ROWS
    pair = 2 * NA_HEAD_DIM

    def bias_idx(b, hp, g):
        kind = jnp.where(g == 0, 0, jnp.where(g == ng - 1, 2, 1))
        return (kind, hp, 0, 0)

    return pl.pallas_call(
        functools.partial(_na_body, rows=rows),
        grid=(B, NA_HEADS // 2, ng),
        in_specs=[
            pl.BlockSpec((1, pair, nq), lambda b, hp, g: (b, hp, g)),
            pl.BlockSpec((1, S, pair), lambda b, hp, g: (b, 0, hp)),
            pl.BlockSpec((1, S // NA_V_CHUNK, pair, NA_V_CHUNK), lambda b, hp, g: (b, 0, hp, 0)),
            pl.BlockSpec((1, 2, nk, nq), bias_idx),
        ],
        out_specs=pl.BlockSpec((1, pair, nq), lambda b, hp, g: (b, hp, g)),
        out_shape=jax.ShapeDtypeStruct((B, NA_WIDTH, S), BF16),
        compiler_params=pltpu.CompilerParams(
            dimension_semantics=("arbitrary", "arbitrary", "arbitrary"), vmem_limit_bytes=VMEM_LIMIT_BYTES),
        name="natten",
    )(qT, k, vT, biasT)


def _na_bias_table(rpb):
    n_off_r, n_off_c = 2 * NA_WIN_H - 1, 2 * NA_WIN_W - 1
    cols = np.arange(GRID_W)
    cs = np.clip(cols - NA_WIN_W // 2, 0, GRID_W - NA_WIN_W)
    col_ok = (cols[:, None] >= cs[None, :]) & (cols[:, None] < cs[None, :] + NA_WIN_W)
    col_idx = np.clip(cols[:, None] - cols[None, :] + NA_WIN_W - 1, 0, n_off_c - 1)
    onehot = (col_idx[None] == np.arange(n_off_c)[:, None, None]) & col_ok[None]
    onehot = jnp.asarray(onehot.reshape(n_off_c, GRID_W * GRID_W), F32)
    blocks = jnp.dot(rpb.reshape(NA_HEADS * n_off_r, n_off_c), onehot, precision=lax.Precision.HIGHEST)
    blocks = blocks.reshape(NA_HEADS, n_off_r, GRID_W, GRID_W)
    blocks = jnp.where(jnp.asarray(col_ok)[None, None], blocks, MASK_VALUE)
    masked = jnp.full((NA_HEADS, 1, GRID_W, GRID_W), MASK_VALUE, F32)
    blocks = jnp.concatenate([blocks, masked], axis=1)

    rows = 4 * NA_KEY_ROWS
    kinds = ((0, 0), (NA_KEY_ROWS, NA_KEY_ROWS - NA_WIN_H // 2), (rows - NA_GROUP_ROWS, rows - NA_KEY_ROWS))
    sel = np.zeros((3, NA_KEY_ROWS, NA_GROUP_ROWS), np.int32)
    for t, (r0, ws) in enumerate(kinds):
        q_row = r0 + np.arange(NA_GROUP_ROWS)
        k_row = ws + np.arange(NA_KEY_ROWS)
        rs = np.clip(q_row - NA_WIN_H // 2, 0, rows - NA_WIN_H)
        row_ok = (k_row[:, None] >= rs[None, :]) & (k_row[:, None] < rs[None, :] + NA_WIN_H)
        sel[t] = np.where(row_ok, k_row[:, None] - q_row[None, :] + NA_WIN_H - 1, n_off_r)
    table = jnp.take(blocks, jnp.asarray(sel.reshape(-1)), axis=1)
    table = table.reshape(NA_HEADS, 3, NA_KEY_ROWS, NA_GROUP_ROWS, GRID_W, GRID_W)
    table = table.transpose(1, 0, 2, 4, 3, 5)
    return table.reshape(3, NA_HEADS, NA_KEY_ROWS * GRID_W, NA_GROUP_ROWS * GRID_W)


def _mla_body(qT_ref, k_ref, vT_ref, o_ref, s0_ref, s1_ref, *, unroll):
    tq = qT_ref.shape[2]
    tk = vT_ref.shape[3]
    nchunk = vT_ref.shape[1]
    qT = qT_ref[0]

    def scores_into(j, s_ref):
        s = _dot(k_ref[0, pl.ds(pl.multiple_of(j * tk, tk), tk), :], qT)
        s_ref[...] = s
        return jnp.max(s, axis=0, keepdims=True)

    def softmax_pv(j, s_ref, mx, m, acc):
        m_new = jnp.maximum(m, mx)
        alpha = jnp.exp2(m - m_new)
        p = jnp.exp2(s_ref[...] - m_new).astype(BF16)
        return m_new, alpha * acc + _dot(vT_ref[0, j], p)

    bufs = (s0_ref, s1_ref)

    def trip(i, carry):
        m, acc, mx = carry
        j0 = unroll * i
        for u in range(unroll):
            j_next = j0 + u + 1
            if u == unroll - 1:
                j_next = jnp.minimum(j_next, nchunk - 1)
            mx_next = scores_into(j_next, bufs[(u + 1) % 2])
            m, acc = softmax_pv(j0 + u, bufs[u % 2], mx, m, acc)
            mx = mx_next
        return m, acc, mx

    mx0 = scores_into(0, s0_ref)
    init = (jnp.full((1, tq), -jnp.inf, F32), jnp.zeros((MLA_V_PAD, tq), F32), mx0)
    _, acc, _ = lax.fori_loop(0, nchunk // unroll, trip, init)
    o_ref[0] = (acc[:MLA_V] / acc[MLA_V:MLA_V + 1]).astype(BF16)


def _mla(qT, k, vT):
    B, _, S = qT.shape
    tq = MLA_TQ
    nchunk, tk = vT.shape[1], vT.shape[3]
    unroll = next(u for u in range(MLA_UNROLL, 0, -2) if nchunk % u == 0)
    return pl.pallas_call(
        functools.partial(_mla_body, unroll=unroll),
        grid=(B, MLA_HEADS, S // tq),
        in_specs=[
            pl.BlockSpec((1, MLA_QK_PAD, tq), lambda b, h, i: (b, h, i)),
            pl.BlockSpec((1, S, MLA_QK_PAD), lambda b, h, i: (b, 0, h)),
            pl.BlockSpec((1, nchunk, MLA_V_PAD, tk), lambda b, h, i: (b, 0, h, 0)),
        ],
        out_specs=pl.BlockSpec((1, MLA_V, tq), lambda b, h, i: (b, h, i)),
        out_shape=jax.ShapeDtypeStruct((B, MLA_WIDTH, S), BF16),
        scratch_shapes=[pltpu.VMEM((tk, tq), F32), pltpu.VMEM((tk, tq), F32)],
        compiler_params=pltpu.CompilerParams(
            dimension_semantics=("arbitrary", "arbitrary", "arbitrary"), vmem_limit_bytes=VMEM_LIMIT_BYTES),
        name="mla",
    )(qT, k, vT)


def _post_body(x_ref, oTna_ref, oTmla_ref, gates_ref, p_ref,
               wnao_ref, wmlao_ref, wout_ref, gffn_ref, wgate_ref, wup_ref, wdown_ref,
               gple_ref, wpleg_ref, wple_ref, gfin_ref, y_ref):
    d = x_ref.shape[2]
    a = _dot_tn(oTna_ref[0], wnao_ref[...])
    b = _dot_tn(oTmla_ref[0], wmlao_ref[...])
    gates = gates_ref[0]
    mix = (gates[:, :d] * a + gates[:, d:] * b).astype(BF16)
    x1 = x_ref[0] + _dot(mix, wout_ref[...])

    hn = _rms(x1, gffn_ref[...]).astype(BF16)
    gate = _dot(hn, wgate_ref[...])
    up = _dot(hn, wup_ref[...])
    act = (gate * jax.nn.sigmoid(gate) * up).astype(BF16)
    x2 = x1 + _dot(act, wdown_ref[...])

    ple_gate = jax.nn.sigmoid(_dot(_rms(x2, gple_ref[...]).astype(BF16), wpleg_ref[...]))
    x3 = x2 + ple_gate * _dot(p_ref[0].astype(BF16), wple_ref[...])
    y_ref[0] = _rms(x3, gfin_ref[...])


def _post(x, oTna, oTmla, gates, p, w):
    B, S, D = x.shape
    tm = TM_POST
    weights = (w["w_na_o"], w["w_mla_o"], w["w_out"], w["g_ffn"], w["w_gate"], w["w_up"], w["w_down"],
               w["g_ple"], w["w_ple_gate"], w["w_ple"], w["g_final"])
    in_specs = [
        pl.BlockSpec((1, tm, D), lambda b, i: (b, i, 0)),
        pl.BlockSpec((1, NA_WIDTH, tm), lambda b, i: (b, 0, i)),
        pl.BlockSpec((1, MLA_WIDTH, tm), lambda b, i: (b, 0, i)),
        pl.BlockSpec((1, tm, 2 * D), lambda b, i: (b, i, 0)),
        pl.BlockSpec((1, tm, PLE_DIM), lambda b, i: (b, i, 0)),
    ]
    in_specs += [_const_spec(a.shape) for a in weights]
    return pl.pallas_call(
        _post_body,
        grid=(B, S // tm),
        in_specs=in_specs,
        out_specs=pl.BlockSpec((1, tm, D), lambda b, i: (b, i, 0)),
        out_shape=jax.ShapeDtypeStruct((B, S, D), F32),
        compiler_params=pltpu.CompilerParams(
            dimension_semantics=("arbitrary", "arbitrary"), vmem_limit_bytes=VMEM_LIMIT_BYTES),
        name="post",
    )(x, oTna, oTmla, gates, p, *weights)


def _prep_weights(g_mix, w_in, g_q_lat, w_uq, g_kv_lat, w_ukv, w_na_o, w_mla_o, w_out,
                  g_ffn, w_gate_up, w_down, g_ple, w_ple_gate, w_ple, g_final):
    o = np.cumsum((0, NA_WIDTH, NA_WIDTH, NA_WIDTH, Q_LORA, KV_LORA, MLA_ROPE, 2 * D_MODEL))
    w_q, w_k, w_v, w_zq, w_zkv, w_zr, w_g = (w_in[:, o[i]:o[i + 1]] for i in range(7))
    half = MLA_ROPE // 2
    w_zr_rot = jnp.concatenate([-w_zr[:, half:], w_zr[:, :half]], axis=1)

    w_uq_h = w_uq.reshape(Q_LORA, MLA_HEADS, MLA_NOPE + MLA_ROPE)
    w_uq_pad = jnp.pad(w_uq_h, ((0, 0), (0, 0), (0, MLA_QK_PAD - MLA_NOPE - MLA_ROPE)))
    w_uqT = w_uq_pad.reshape(Q_LORA, MLA_HEADS * MLA_QK_PAD).T

    w_ukv_h = w_ukv.reshape(KV_LORA, MLA_HEADS, MLA_NOPE + MLA_V)
    w_uk = jnp.pad(w_ukv_h[:, :, :MLA_NOPE], ((0, 0), (0, 0), (0, MLA_QK_PAD - MLA_NOPE)))
    w_uk = w_uk.reshape(KV_LORA, MLA_HEADS * MLA_QK_PAD)
    w_uvT = w_ukv_h[:, :, MLA_NOPE:].reshape(KV_LORA, MLA_WIDTH).T

    place = np.zeros((MLA_ROPE, MLA_HEADS, MLA_QK_PAD), np.float32)
    for j in range(MLA_ROPE):
        place[j, :, MLA_NOPE + j] = 1.0
    place = place.reshape(MLA_ROPE, MLA_HEADS * MLA_QK_PAD)

    w_gate = w_gate_up[:, :D_FF]
    w_up = w_gate_up[:, D_FF:]
    w_dn = w_down

    row = lambda g: g.reshape(1, -1).astype(F32)
    return {
        "g_mix": row(g_mix),
        "w_kna": w_k.astype(BF16),
        "w_qnaT": (w_q * NA_HEAD_DIM ** -0.5).T.astype(BF16),
        "w_vnaT": w_v.T.astype(BF16),
        "w_zq": w_zq.astype(BF16),
        "w_zkv": w_zkv.astype(BF16),
        "w_zrr": jnp.concatenate([w_zr, w_zr_rot], axis=1).astype(BF16),
        "w_g": w_g.astype(BF16),
        "g_q": row(g_q_lat),
        "w_uqT": w_uqT.astype(BF16),
        "g_kv": row(g_kv_lat),
        "w_uk": w_uk.astype(BF16),
        "w_uvT": w_uvT.astype(BF16),
        "place": jnp.asarray(place, BF16),
        "w_na_o": w_na_o.astype(BF16),
        "w_mla_o": w_mla_o.astype(BF16),
        "w_out": w_out.astype(BF16),
        "g_ffn": row(g_ffn),
        "w_gate": w_gate.astype(BF16),
        "w_up": w_up.astype(BF16),
        "w_down": w_dn.astype(BF16),
        "g_ple": row(g_ple),
        "w_ple_gate": w_ple_gate.astype(BF16),
        "w_ple": w_ple.astype(BF16),
        "g_final": row(g_final),
    }


def _rope_tables(s_max):
    half = MLA_ROPE // 2
    pos = jnp.arange(s_max, dtype=F32)
    freqs = ROPE_THETA ** (-jnp.arange(half, dtype=F32) / half)
    ang = pos[:, None] * freqs[None, :]
    cos, sin = jnp.cos(ang), jnp.sin(ang)
    return {"cosT": cos.T, "sinT": sin.T,
            "cos": jnp.concatenate([cos, cos], axis=1), "sin": jnp.concatenate([sin, sin], axis=1)}


def _trunk(x, p, w, tabs, biasT):
    kna, qTna, vTna, gates, qT, k, vT = _inproj(x, w, tabs)
    oTna = _na(qTna, kna, vTna, biasT)
    oTmla = _mla(qT, k, vT)
    return _post(x, oTna, oTmla, gates, p, w)


def kernel(x_prompt, x_sample, p_prompt, p_sample, g_mix, w_in, rpb, g_q_lat, w_uq, g_kv_lat, w_ukv,
           w_na_o, w_mla_o, w_out, g_ffn, w_gate_up, w_down, g_ple, w_ple_gate, w_ple, g_final):
    assert g_mix.shape[0] == 1, "single-layer trunk"
    w = _prep_weights(g_mix[0], w_in[0], g_q_lat[0], w_uq[0], g_kv_lat[0], w_ukv[0], w_na_o[0], w_mla_o[0],
                      w_out[0], g_ffn[0], w_gate_up[0], w_down[0], g_ple[0], w_ple_gate[0], w_ple[0], g_final)
    biasT = _na_bias_table(rpb[0])
    tabs = _rope_tables(max(x_prompt.shape[1], x_sample.shape[1]))
    y_prompt = _trunk(x_prompt, p_prompt[0], w, tabs, biasT)
    y_sample = _trunk(x_sample, p_sample[0], w, tabs, biasT)
    return (y_prompt, y_sample)
```

```python
import functools

import numpy as np
import jax
import jax.numpy as jnp
from jax import lax
from jax.experimental import pallas as pl
from jax.experimental.pallas import tpu as pltpu

D_MODEL = 1024
GRID_W = 64
NA_HEADS = 8
NA_HEAD_DIM = 64
NA_WIN_H = 8
NA_WIN_W = 16
MLA_HEADS = 8
MLA_NOPE = 64
MLA_ROPE = 32
MLA_V = 64
Q_LORA = 384
KV_LORA = 256
ROPE_THETA = 10000.0
D_FF = 2816
PLE_DIM = 256
EPS = 1e-6
NA_WIDTH = NA_HEADS * NA_HEAD_DIM
MLA_WIDTH = MLA_HEADS * MLA_V

MLA_QK_PAD = 128
MLA_V_PAD = 80
NA_V_PAD = 80
LOG2_E = 1.4426950408889634
NA_GROUP_ROWS = 8
NA_KEY_ROWS = 16
NA_V_CHUNK = 256
MASK_VALUE = -1e30

TM_IN = 256
TM_POST = 256
MLA_TQ = 512
MLA_TK = 256
MLA_UNROLL = 32
MLA_SCORE_BUFS = 4
MLA_LOOKAHEAD = 2
VMEM_LIMIT_BYTES = 56 * 1024 * 1024

BF16 = jnp.bfloat16
F32 = jnp.float32


def _dot(a, b):
    return jnp.dot(a, b, preferred_element_type=F32)


def _dot_nt(a, b):
    return lax.dot_general(a, b, (((1,), (1,)), ((), ())), preferred_element_type=F32)


def _dot_tn(a, b):
    return lax.dot_general(a, b, (((0,), (0,)), ((), ())), preferred_element_type=F32)


def _rms(x, g):
    ms = jnp.mean(x * x, axis=-1, keepdims=True)
    return x * lax.rsqrt(ms + EPS) * g


def _const_spec(shape):
    nd = len(shape)
    return pl.BlockSpec(shape, lambda *_: (0,) * nd, pipeline_mode=pl.Buffered(1))


def _inproj_body(x_ref, gmix_ref, wkna_ref, wqnaT_ref, wvnaT_ref, wzq_ref, wzkv_ref, wzrr_ref, wg_ref,
                 gq_ref, wuqT_ref, gkv_ref, wuk_ref, wuvT_ref, place_ref,
                 cosT_ref, sinT_ref, cos_ref, sin_ref,
                 kna_ref, qTna_ref, vTna_ref, gates_ref, qT_ref, k_ref, vT_ref):
    tm = x_ref.shape[1]
    h = _rms(x_ref[0], gmix_ref[...]).astype(BF16)

    kna_ref[0] = _dot(h, wkna_ref[...]).astype(BF16)
    qTna_ref[0] = (_dot_nt(wqnaT_ref[...], h) * LOG2_E).astype(BF16)
    vTna = _dot_nt(wvnaT_ref[...], h).astype(BF16)
    ones_na = (lax.broadcasted_iota(jnp.int32, (NA_V_PAD - NA_HEAD_DIM, NA_V_CHUNK), 0) == 0).astype(F32).astype(BF16)
    for c in range(tm // NA_V_CHUNK):
        for hh in range(NA_HEADS):
            vTna_ref[0, c, hh * NA_V_PAD:hh * NA_V_PAD + NA_HEAD_DIM, :] = (
                vTna[hh * NA_HEAD_DIM:(hh + 1) * NA_HEAD_DIM, c * NA_V_CHUNK:(c + 1) * NA_V_CHUNK])
            vTna_ref[0, c, hh * NA_V_PAD + NA_HEAD_DIM:(hh + 1) * NA_V_PAD, :] = ones_na

    gates_ref[0] = jax.nn.sigmoid(_dot(h, wg_ref[...]))

    zqn = _rms(_dot(h, wzq_ref[...]), gq_ref[...]).astype(BF16)
    zkvn = _rms(_dot(h, wzkv_ref[...]), gkv_ref[...]).astype(BF16)

    scale = (MLA_NOPE + MLA_ROPE) ** -0.5 * LOG2_E
    qT = _dot_nt(wuqT_ref[...], zqn)
    cT = cosT_ref[...]
    sT = sinT_ref[...]
    half = MLA_ROPE // 2
    for hh in range(MLA_HEADS):
        base = hh * MLA_QK_PAD
        nope = qT[base:base + MLA_NOPE]
        x1 = qT[base + MLA_NOPE:base + MLA_NOPE + half]
        x2 = qT[base + MLA_NOPE + half:base + MLA_NOPE + MLA_ROPE]
        qT_ref[0, 0, base:base + MLA_NOPE, :] = (nope * scale).astype(BF16)
        qT_ref[0, 0, base + MLA_NOPE:base + MLA_NOPE + half, :] = ((x1 * cT - x2 * sT) * scale).astype(BF16)
        qT_ref[0, 0, base + MLA_NOPE + half:base + MLA_NOPE + MLA_ROPE, :] = ((x2 * cT + x1 * sT) * scale).astype(BF16)
        qT_ref[0, 0, base + MLA_NOPE + MLA_ROPE:base + MLA_QK_PAD, :] = jnp.zeros(
            (MLA_QK_PAD - MLA_NOPE - MLA_ROPE, tm), BF16)

    zrr = _dot(h, wzrr_ref[...])
    k_rope = zrr[:, :MLA_ROPE] * cos_ref[...] + zrr[:, MLA_ROPE:] * sin_ref[...]
    k_nope = _dot(zkvn, wuk_ref[...])
    k_ref[0] = (k_nope + _dot(k_rope.astype(BF16), place_ref[...])).astype(BF16)

    vT = _dot_nt(wuvT_ref[...], zkvn).astype(BF16)
    pad_rows = MLA_V_PAD - MLA_V
    ones_row = (lax.broadcasted_iota(jnp.int32, (pad_rows, tm), 0) == 0).astype(F32).astype(BF16)
    for hh in range(MLA_HEADS):
        vT_ref[0, 0, hh * MLA_V_PAD:hh * MLA_V_PAD + MLA_V, :] = vT[hh * MLA_V:(hh + 1) * MLA_V]
        vT_ref[0, 0, hh * MLA_V_PAD + MLA_V:(hh + 1) * MLA_V_PAD, :] = ones_row


def _inproj(x, w, tabs):
    B, S, D = x.shape
    tm = TM_IN
    nt = S // tm
    sub = MLA_TK // tm
    subq = MLA_TQ // tm
    hq = MLA_HEADS * MLA_QK_PAD
    weights = (w["g_mix"], w["w_kna"], w["w_qnaT"], w["w_vnaT"], w["w_zq"], w["w_zkv"], w["w_zrr"], w["w_g"],
               w["g_q"], w["w_uqT"], w["g_kv"], w["w_uk"], w["w_uvT"], w["place"])
    in_specs = [pl.BlockSpec((1, tm, D), lambda b, i: (b, i, 0))]
    in_specs += [_const_spec(a.shape) for a in weights]
    in_specs += [
        pl.BlockSpec((MLA_ROPE // 2, tm), lambda b, i: (0, i)),
        pl.BlockSpec((MLA_ROPE // 2, tm), lambda b, i: (0, i)),
        pl.BlockSpec((tm, MLA_ROPE), lambda b, i: (i, 0)),
        pl.BlockSpec((tm, MLA_ROPE), lambda b, i: (i, 0)),
    ]
    out_shape = (
        jax.ShapeDtypeStruct((B, S, NA_WIDTH), BF16),
        jax.ShapeDtypeStruct((B, NA_WIDTH, S), BF16),
        jax.ShapeDtypeStruct((B, S // NA_V_CHUNK, NA_HEADS * NA_V_PAD, NA_V_CHUNK), BF16),
        jax.ShapeDtypeStruct((B, S, 2 * D), F32),
        jax.ShapeDtypeStruct((B, S // MLA_TQ, hq, MLA_TQ), BF16),
        jax.ShapeDtypeStruct((B, S, hq), BF16),
        jax.ShapeDtypeStruct((B, S // MLA_TK, MLA_HEADS * MLA_V_PAD, MLA_TK), BF16),
    )
    out_specs = (
        pl.BlockSpec((1, tm, NA_WIDTH), lambda b, i: (b, i, 0)),
        pl.BlockSpec((1, NA_WIDTH, tm), lambda b, i: (b, 0, i)),
        pl.BlockSpec((1, tm // NA_V_CHUNK, NA_HEADS * NA_V_PAD, NA_V_CHUNK), lambda b, i: (b, i, 0, 0)),
        pl.BlockSpec((1, tm, 2 * D), lambda b, i: (b, i, 0)),
        pl.BlockSpec((1, 1, hq, tm), lambda b, i: (b, i // subq, 0, i % subq)),
        pl.BlockSpec((1, tm, hq), lambda b, i: (b, i, 0)),
        pl.BlockSpec((1, 1, MLA_HEADS * MLA_V_PAD, tm), lambda b, i: (b, i // sub, 0, i % sub)),
    )
    return pl.pallas_call(
        _inproj_body,
        grid=(B, nt),
        in_specs=in_specs,
        out_specs=out_specs,
        out_shape=out_shape,
        compiler_params=pltpu.CompilerParams(
            dimension_semantics=("arbitrary", "arbitrary"), vmem_limit_bytes=VMEM_LIMIT_BYTES),
        name="inproj",
    )(x, *weights, tabs["cosT"][:, :S], tabs["sinT"][:, :S], tabs["cos"][:S], tabs["sin"][:S])


def _na_body(qT_ref, k_ref, vT_ref, bias_ref, o_ref, *s_refs, rows):
    g = pl.program_id(2)
    nq = NA_GROUP_ROWS * GRID_W
    nk = NA_KEY_ROWS * GRID_W
    ws = jnp.clip(NA_GROUP_ROWS * g - NA_WIN_H // 2, 0, rows - NA_KEY_ROWS)
    tok0 = pl.multiple_of(ws * GRID_W, NA_V_CHUNK)
    c0 = lax.shift_right_logical(ws, 2)
    kwin = k_ref[0, pl.ds(tok0, nk), :]
    qT = qT_ref[0]
    row = lax.broadcasted_iota(jnp.int32, qT.shape, 0)
    maxes = []
    for hh in range(2):
        lo = hh * NA_HEAD_DIM
        q_h = jnp.where((row >= lo) & (row < lo + NA_HEAD_DIM), qT, jnp.zeros_like(qT))
        s = _dot(kwin, q_h) + bias_ref[0, hh]
        s_refs[hh][...] = s
        maxes.append(jnp.max(s, axis=0, keepdims=True))
    for hh in range(2):
        lo = hh * NA_HEAD_DIM
        pb = jnp.exp2(s_refs[hh][...] - maxes[hh]).astype(BF16)
        acc = jnp.zeros((NA_V_PAD, nq), F32)
        for c in range(nk // NA_V_CHUNK):
            v_c = vT_ref[0, c0 + c, hh * NA_V_PAD:(hh + 1) * NA_V_PAD, :]
            acc = acc + _dot(v_c, pb[c * NA_V_CHUNK:(c + 1) * NA_V_CHUNK])
        o_ref[0, lo:lo + NA_HEAD_DIM, :] = (acc[:NA_HEAD_DIM] / acc[NA_HEAD_DIM:NA_HEAD_DIM + 1]).astype(BF16)


def _na(qT, k, vT, biasT):
    B, _, S = qT.shape
    rows = S // GRID_W
    nq = NA_GROUP_ROWS * GRID_W
    nk = NA_KEY_ROWS * GRID_W
    ng = rows // NA_GROUP_ROWS
    pair = 2 * NA_HEAD_DIM

    def bias_idx(b, hp, g):
        kind = jnp.where(g == 0, 0, jnp.where(g == ng - 1, 2, 1))
        return (kind, hp, 0, 0)

    return pl.pallas_call(
        functools.partial(_na_body, rows=rows),
        grid=(B, NA_HEADS // 2, ng),
        in_specs=[
            pl.BlockSpec((1, pair, nq), lambda b, hp, g: (b, hp, g)),
            pl.BlockSpec((1, S, pair), lambda b, hp, g: (b, 0, hp)),
            pl.BlockSpec((1, S // NA_V_CHUNK, 2 * NA_V_PAD, NA_V_CHUNK), lambda b, hp, g: (b, 0, hp, 0)),
            pl.BlockSpec((1, 2, nk, nq), bias_idx),
        ],
        out_specs=pl.BlockSpec((1, pair, nq), lambda b, hp, g: (b, hp, g)),
        out_shape=jax.ShapeDtypeStruct((B, NA_WIDTH, S), BF16),
        scratch_shapes=[pltpu.VMEM((nk, nq), F32)] * 2,
        compiler_params=pltpu.CompilerParams(
            dimension_semantics=("arbitrary", "arbitrary", "arbitrary"), vmem_limit_bytes=VMEM_LIMIT_BYTES),
        name="natten",
    )(qT, k, vT, biasT)


def _na_bias_table(rpb):
    n_off_r, n_off_c = 2 * NA_WIN_H - 1, 2 * NA_WIN_W - 1
    cols = np.arange(GRID_W)
    cs = np.clip(cols - NA_WIN_W // 2, 0, GRID_W - NA_WIN_W)
    col_ok = (cols[:, None] >= cs[None, :]) & (cols[:, None] < cs[None, :] + NA_WIN_W)
    col_idx = np.clip(cols[:, None] - cols[None, :] + NA_WIN_W - 1, 0, n_off_c - 1)
    onehot = (col_idx[None] == np.arange(n_off_c)[:, None, None]) & col_ok[None]
    onehot = jnp.asarray(onehot.reshape(n_off_c, GRID_W * GRID_W), F32)
    blocks = jnp.dot(rpb.reshape(NA_HEADS * n_off_r, n_off_c), onehot, precision=lax.Precision.HIGHEST)
    blocks = blocks.reshape(NA_HEADS, n_off_r, GRID_W, GRID_W)
    blocks = jnp.where(jnp.asarray(col_ok)[None, None], blocks * LOG2_E, MASK_VALUE)
    masked = jnp.full((NA_HEADS, 1, GRID_W, GRID_W), MASK_VALUE, F32)
    blocks = jnp.concatenate([blocks, masked], axis=1)

    rows = 4 * NA_KEY_ROWS
    kinds = ((0, 0), (NA_KEY_ROWS, NA_KEY_ROWS - NA_WIN_H // 2), (rows - NA_GROUP_ROWS, rows - NA_KEY_ROWS))
    sel = np.zeros((3, NA_KEY_ROWS, NA_GROUP_ROWS), np.int32)
    for t, (r0, ws) in enumerate(kinds):
        q_row = r0 + np.arange(NA_GROUP_ROWS)
        k_row = ws + np.arange(NA_KEY_ROWS)
        rs = np.clip(q_row - NA_WIN_H // 2, 0, rows - NA_WIN_H)
        row_ok = (k_row[:, None] >= rs[None, :]) & (k_row[:, None] < rs[None, :] + NA_WIN_H)
        sel[t] = np.where(row_ok, k_row[:, None] - q_row[None, :] + NA_WIN_H - 1, n_off_r)
    table = jnp.take(blocks, jnp.asarray(sel.reshape(-1)), axis=1)
    table = table.reshape(NA_HEADS, 3, NA_KEY_ROWS, NA_GROUP_ROWS, GRID_W, GRID_W)
    table = table.transpose(1, 0, 2, 4, 3, 5)
    return table.reshape(3, NA_HEADS, NA_KEY_ROWS * GRID_W, NA_GROUP_ROWS * GRID_W)


def _mla_body(qT_ref, k_ref, vT_ref, o_ref, *s_refs, unroll):
    nq, tq = qT_ref.shape[1], qT_ref.shape[3]
    nchunk, tk = vT_ref.shape[1], vT_ref.shape[3]
    nbuf = len(s_refs)
    nsteps = nq * nchunk

    def scores_into(qb, j, s_ref):
        s = _dot(k_ref[0, pl.ds(pl.multiple_of(j * tk, tk), tk), :], qT_ref[0, qb])
        s_ref[...] = s
        return jnp.max(s, axis=0, keepdims=True)

    def softmax_pv(j, s_ref, mx, m, acc):
        m_new = jnp.maximum(m, mx)
        alpha = jnp.exp2(m - m_new)
        p = jnp.exp2(s_ref[...] - m_new).astype(BF16)
        return m_new, alpha * acc + _dot(vT_ref[0, j], p)

    def trip(i, carry):
        m, acc, mxs = carry
        t0 = i * unroll
        qb = lax.div(t0, nchunk)
        j0 = t0 - qb * nchunk
        t1 = jnp.minimum(t0 + unroll, nsteps - unroll)
        qb1 = lax.div(t1, nchunk)
        j1 = t1 - qb1 * nchunk
        for u in range(unroll):
            ahead = u + MLA_LOOKAHEAD
            if ahead < unroll:
                mx_new = scores_into(qb, j0 + ahead, s_refs[ahead % nbuf])
            else:
                mx_new = scores_into(qb1, j1 + ahead - unroll, s_refs[ahead % nbuf])
            m, acc = softmax_pv(j0 + u, s_refs[u % nbuf], mxs[0], m, acc)
            mxs = mxs[1:] + (mx_new,)
        done = j0 + unroll == nchunk

        @pl.when(done)
        def _():
            o_ref[0, qb] = (acc[:MLA_V] / acc[MLA_V:MLA_V + 1]).astype(BF16)

        m = jnp.where(done, -jnp.inf, m)
        acc = jnp.where(done, 0.0, acc)
        return m, acc, mxs

    mxs = tuple(scores_into(0, t, s_refs[t]) for t in range(MLA_LOOKAHEAD))
    init = (jnp.full((1, tq), -jnp.inf, F32), jnp.zeros((MLA_V_PAD, tq), F32), mxs)
    lax.fori_loop(0, nsteps // unroll, trip, init)


def _mla(qT, k, vT):
    B, nq, _, tq = qT.shape
    S = k.shape[1]
    nchunk, tk = vT.shape[1], vT.shape[3]
    unroll = next(u for u in range(MLA_UNROLL, 0, -MLA_SCORE_BUFS) if nchunk % u == 0)
    assert MLA_UNROLL % MLA_SCORE_BUFS == 0 and MLA_LOOKAHEAD < MLA_SCORE_BUFS <= unroll
    return pl.pallas_call(
        functools.partial(_mla_body, unroll=unroll),
        grid=(B, MLA_HEADS),
        in_specs=[
            pl.BlockSpec((1, nq, MLA_QK_PAD, tq), lambda b, h: (b, 0, h, 0)),
            pl.BlockSpec((1, S, MLA_QK_PAD), lambda b, h: (b, 0, h)),
            pl.BlockSpec((1, nchunk, MLA_V_PAD, tk), lambda b, h: (b, 0, h, 0)),
        ],
        out_specs=pl.BlockSpec((1, nq, MLA_V, tq), lambda b, h: (b, 0, h, 0)),
        out_shape=jax.ShapeDtypeStruct((B, nq, MLA_WIDTH, tq), BF16),
        scratch_shapes=[pltpu.VMEM((tk, tq), F32)] * MLA_SCORE_BUFS,
        compiler_params=pltpu.CompilerParams(
            dimension_semantics=("arbitrary", "arbitrary"), vmem_limit_bytes=VMEM_LIMIT_BYTES),
        name="mla",
    )(qT, k, vT)


def _post_body(x_ref, oTna_ref, oTmla_ref, gates_ref, p_ref,
               wnao_ref, wmlao_ref, wout_ref, gffn_ref, wgate_ref, wup_ref, wdown_ref,
               gple_ref, wpleg_ref, wple_ref, gfin_ref, y_ref):
    d = x_ref.shape[2]
    a = _dot_tn(oTna_ref[0], wnao_ref[...])
    b = _dot_tn(oTmla_ref[0, 0], wmlao_ref[...])
    gates = gates_ref[0]
    mix = (gates[:, :d] * a + gates[:, d:] * b).astype(BF16)
    x1 = x_ref[0] + _dot(mix, wout_ref[...])

    hn = _rms(x1, gffn_ref[...]).astype(BF16)
    gate = _dot(hn, wgate_ref[...])
    up = _dot(hn, wup_ref[...])
    act = (gate * jax.nn.sigmoid(gate) * up).astype(BF16)
    x2 = x1 + _dot(act, wdown_ref[...])

    ple_gate = jax.nn.sigmoid(_dot(_rms(x2, gple_ref[...]).astype(BF16), wpleg_ref[...]))
    x3 = x2 + ple_gate * _dot(p_ref[0].astype(BF16), wple_ref[...])
    y_ref[0] = _rms(x3, gfin_ref[...])


def _post(x, oTna, oTmla, gates, p, w):
    B, S, D = x.shape
    tm = TM_POST
    subq = MLA_TQ // tm
    weights = (w["w_na_o"], w["w_mla_o"], w["w_out"], w["g_ffn"], w["w_gate"], w["w_up"], w["w_down"],
               w["g_ple"], w["w_ple_gate"], w["w_ple"], w["g_final"])
    in_specs = [
        pl.BlockSpec((1, tm, D), lambda b, i: (b, i, 0)),
        pl.BlockSpec((1, NA_WIDTH, tm), lambda b, i: (b, 0, i)),
        pl.BlockSpec((1, 1, MLA_WIDTH, tm), lambda b, i: (b, i // subq, 0, i % subq)),
        pl.BlockSpec((1, tm, 2 * D), lambda b, i: (b, i, 0)),
        pl.BlockSpec((1, tm, PLE_DIM), lambda b, i: (b, i, 0)),
    ]
    in_specs += [_const_spec(a.shape) for a in weights]
    return pl.pallas_call(
        _post_body,
        grid=(B, S // tm),
        in_specs=in_specs,
        out_specs=pl.BlockSpec((1, tm, D), lambda b, i: (b, i, 0)),
        out_shape=jax.ShapeDtypeStruct((B, S, D), F32),
        compiler_params=pltpu.CompilerParams(
            dimension_semantics=("arbitrary", "arbitrary"), vmem_limit_bytes=VMEM_LIMIT_BYTES),
        name="post",
    )(x, oTna, oTmla, gates, p, *weights)


def _prep_weights(g_mix, w_in, g_q_lat, w_uq, g_kv_lat, w_ukv, w_na_o, w_mla_o, w_out,
                  g_ffn, w_gate_up, w_down, g_ple, w_ple_gate, w_ple, g_final):
    o = np.cumsum((0, NA_WIDTH, NA_WIDTH, NA_WIDTH, Q_LORA, KV_LORA, MLA_ROPE, 2 * D_MODEL))
    w_q, w_k, w_v, w_zq, w_zkv, w_zr, w_g = (w_in[:, o[i]:o[i + 1]] for i in range(7))
    half = MLA_ROPE // 2
    w_zr_rot = jnp.concatenate([-w_zr[:, half:], w_zr[:, :half]], axis=1)

    w_uq_h = w_uq.reshape(Q_LORA, MLA_HEADS, MLA_NOPE + MLA_ROPE)
    w_uq_pad = jnp.pad(w_uq_h, ((0, 0), (0, 0), (0, MLA_QK_PAD - MLA_NOPE - MLA_ROPE)))
    w_uqT = w_uq_pad.reshape(Q_LORA, MLA_HEADS * MLA_QK_PAD).T

    w_ukv_h = w_ukv.reshape(KV_LORA, MLA_HEADS, MLA_NOPE + MLA_V)
    w_uk = jnp.pad(w_ukv_h[:, :, :MLA_NOPE], ((0, 0), (0, 0), (0, MLA_QK_PAD - MLA_NOPE)))
    w_uk = w_uk.reshape(KV_LORA, MLA_HEADS * MLA_QK_PAD)
    w_uvT = w_ukv_h[:, :, MLA_NOPE:].reshape(KV_LORA, MLA_WIDTH).T

    place = np.zeros((MLA_ROPE, MLA_HEADS, MLA_QK_PAD), np.float32)
    for j in range(MLA_ROPE):
        place[j, :, MLA_NOPE + j] = 1.0
    place = place.reshape(MLA_ROPE, MLA_HEADS * MLA_QK_PAD)

    w_gate = w_gate_up[:, :D_FF]
    w_up = w_gate_up[:, D_FF:]
    w_dn = w_down

    row = lambda g: g.reshape(1, -1).astype(F32)
    return {
        "g_mix": row(g_mix),
        "w_kna": w_k.astype(BF16),
        "w_qnaT": (w_q * NA_HEAD_DIM ** -0.5).T.astype(BF16),
        "w_vnaT": w_v.T.astype(BF16),
        "w_zq": w_zq.astype(BF16),
        "w_zkv": w_zkv.astype(BF16),
        "w_zrr": jnp.concatenate([w_zr, w_zr_rot], axis=1).astype(BF16),
        "w_g": w_g.astype(BF16),
        "g_q": row(g_q_lat),
        "w_uqT": w_uqT.astype(BF16),
        "g_kv": row(g_kv_lat),
        "w_uk": w_uk.astype(BF16),
        "w_uvT": w_uvT.astype(BF16),
        "place": jnp.asarray(place, BF16),
        "w_na_o": w_na_o.astype(BF16),
        "w_mla_o": w_mla_o.astype(BF16),
        "w_out": w_out.astype(BF16),
        "g_ffn": row(g_ffn),
        "w_gate": w_gate.astype(BF16),
        "w_up": w_up.astype(BF16),
        "w_down": w_dn.astype(BF16),
        "g_ple": row(g_ple),
        "w_ple_gate": w_ple_gate.astype(BF16),
        "w_ple": w_ple.astype(BF16),
        "g_final": row(g_final),
    }


def _rope_tables(s_max):
    half = MLA_ROPE // 2
    pos = jnp.arange(s_max, dtype=F32)
    freqs = ROPE_THETA ** (-jnp.arange(half, dtype=F32) / half)
    ang = pos[:, None] * freqs[None, :]
    cos, sin = jnp.cos(ang), jnp.sin(ang)
    return {"cosT": cos.T, "sinT": sin.T,
            "cos": jnp.concatenate([cos, cos], axis=1), "sin": jnp.concatenate([sin, sin], axis=1)}


def _trunk(x, p, w, tabs, biasT):
    kna, qTna, vTna, gates, qT, k, vT = _inproj(x, w, tabs)
    oTna = _na(qTna, kna, vTna, biasT)
    oTmla = _mla(qT, k, vT)
    return _post(x, oTna, oTmla, gates, p, w)


def kernel(x_prompt, x_sample, p_prompt, p_sample, g_mix, w_in, rpb, g_q_lat, w_uq, g_kv_lat, w_ukv,
           w_na_o, w_mla_o, w_out, g_ffn, w_gate_up, w_down, g_ple, w_ple_gate, w_ple, g_final):
    assert g_mix.shape[0] == 1, "single-layer trunk"
    w = _prep_weights(g_mix[0], w_in[0], g_q_lat[0], w_uq[0], g_kv_lat[0], w_ukv[0], w_na_o[0], w_mla_o[0],
                      w_out[0], g_ffn[0], w_gate_up[0], w_down[0], g_ple[0], w_ple_gate[0], w_ple[0], g_final)
    biasT = _na_bias_table(rpb[0])
    tabs = _rope_tables(max(x_prompt.shape[1], x_sample.shape[1]))
    y_prompt = _trunk(x_prompt, p_prompt[0], w, tabs, biasT)
    y_sample = _trunk(x_sample, p_sample[0], w, tabs, biasT)
    return (y_prompt, y_sample)
```

```python
import functools

import numpy as np
import jax
import jax.numpy as jnp
from jax import lax
from jax.experimental import pallas as pl
from jax.experimental.pallas import tpu as pltpu

D_MODEL = 1024
GRID_W = 64
NA_HEADS = 8
NA_HEAD_DIM = 64
NA_WIN_H = 8
NA_WIN_W = 16
MLA_HEADS = 8
MLA_NOPE = 64
MLA_ROPE = 32
MLA_V = 64
Q_LORA = 384
KV_LORA = 256
ROPE_THETA = 10000.0
D_FF = 2816
PLE_DIM = 256
EPS = 1e-6
NA_WIDTH = NA_HEADS * NA_HEAD_DIM
MLA_WIDTH = MLA_HEADS * MLA_V

MLA_QK_PAD = 128
MLA_V_PAD = 80
NA_V_PAD = 80
LOG2_E = 1.4426950408889634
NA_GROUP_ROWS = 4
NA_KEY_ROWS = 12
NA_UNROLL = 16
NA_SCORE_BUFS = 4
NA_LOOKAHEAD = 2
NA_V_CHUNK = 256
MASK_VALUE = -1e30

TM_IN = 256
TM_POST = 256
MLA_TQ = 512
MLA_TK = 256
MLA_UNROLL = 32
MLA_SCORE_BUFS = 4
MLA_LOOKAHEAD = 2
VMEM_LIMIT_BYTES = 56 * 1024 * 1024

BF16 = jnp.bfloat16
F32 = jnp.float32


def _dot(a, b):
    return jnp.dot(a, b, preferred_element_type=F32)


def _dot_nt(a, b):
    return lax.dot_general(a, b, (((1,), (1,)), ((), ())), preferred_element_type=F32)


def _dot_tn(a, b):
    return lax.dot_general(a, b, (((0,), (0,)), ((), ())), preferred_element_type=F32)


def _rms(x, g):
    ms = jnp.mean(x * x, axis=-1, keepdims=True)
    return x * lax.rsqrt(ms + EPS) * g


def _const_spec(shape):
    nd = len(shape)
    return pl.BlockSpec(shape, lambda *_: (0,) * nd, pipeline_mode=pl.Buffered(1))


def _chunked_spec(rows, tm, chunk):
    if tm >= chunk:
        return pl.BlockSpec((1, tm // chunk, rows, chunk), lambda b, i: (b, i, 0, 0))
    sub = chunk // tm
    return pl.BlockSpec((1, 1, rows, tm), lambda b, i: (b, i // sub, 0, i % sub))


def _store_chunked(ref, r0, val):
    width = ref.shape[3]
    for c in range(ref.shape[1]):
        ref[0, c, r0:r0 + val.shape[0], :] = val[:, c * width:(c + 1) * width]


def _inproj_body(x_ref, gmix_ref, wkna_ref, wqnaT_ref, wvnaT_ref, wzq_ref, wzkv_ref, wzrr_ref, wg_ref,
                 gq_ref, wuqT_ref, gkv_ref, wuk_ref, wuvT_ref, place_ref,
                 cosT_ref, sinT_ref, cos_ref, sin_ref,
                 kna_ref, qTna_ref, vTna_ref, gates_ref, qT_ref, k_ref, vT_ref):
    tm = x_ref.shape[1]
    h = _rms(x_ref[0], gmix_ref[...]).astype(BF16)

    kna_ref[0] = _dot(h, wkna_ref[...]).astype(BF16)
    _store_chunked(qTna_ref, 0, (_dot_nt(wqnaT_ref[...], h) * LOG2_E).astype(BF16))
    vTna = _dot_nt(wvnaT_ref[...], h).astype(BF16)
    ones_rows = (lax.broadcasted_iota(jnp.int32, (NA_V_PAD - NA_HEAD_DIM, tm), 0) == 0).astype(F32).astype(BF16)
    for hh in range(NA_HEADS):
        _store_chunked(vTna_ref, hh * NA_V_PAD, vTna[hh * NA_HEAD_DIM:(hh + 1) * NA_HEAD_DIM])
        _store_chunked(vTna_ref, hh * NA_V_PAD + NA_HEAD_DIM, ones_rows)

    gates_ref[0] = jax.nn.sigmoid(_dot(h, wg_ref[...]))

    zqn = _rms(_dot(h, wzq_ref[...]), gq_ref[...]).astype(BF16)
    zkvn = _rms(_dot(h, wzkv_ref[...]), gkv_ref[...]).astype(BF16)

    scale = (MLA_NOPE + MLA_ROPE) ** -0.5 * LOG2_E
    qT = _dot_nt(wuqT_ref[...], zqn)
    cT = cosT_ref[...]
    sT = sinT_ref[...]
    half = MLA_ROPE // 2
    for hh in range(MLA_HEADS):
        base = hh * MLA_QK_PAD
        nope = qT[base:base + MLA_NOPE]
        x1 = qT[base + MLA_NOPE:base + MLA_NOPE + half]
        x2 = qT[base + MLA_NOPE + half:base + MLA_NOPE + MLA_ROPE]
        _store_chunked(qT_ref, base, (nope * scale).astype(BF16))
        _store_chunked(qT_ref, base + MLA_NOPE, ((x1 * cT - x2 * sT) * scale).astype(BF16))
        _store_chunked(qT_ref, base + MLA_NOPE + half, ((x2 * cT + x1 * sT) * scale).astype(BF16))
        _store_chunked(qT_ref, base + MLA_NOPE + MLA_ROPE, jnp.zeros((MLA_QK_PAD - MLA_NOPE - MLA_ROPE, tm), BF16))

    zrr = _dot(h, wzrr_ref[...])
    k_rope = zrr[:, :MLA_ROPE] * cos_ref[...] + zrr[:, MLA_ROPE:] * sin_ref[...]
    k_nope = _dot(zkvn, wuk_ref[...])
    k_ref[0] = (k_nope + _dot(k_rope.astype(BF16), place_ref[...])).astype(BF16)

    vT = _dot_nt(wuvT_ref[...], zkvn).astype(BF16)
    for hh in range(MLA_HEADS):
        _store_chunked(vT_ref, hh * MLA_V_PAD, vT[hh * MLA_V:(hh + 1) * MLA_V])
        _store_chunked(vT_ref, hh * MLA_V_PAD + MLA_V, ones_rows)


def _inproj(x, w, tabs):
    B, S, D = x.shape
    tm = TM_IN
    nt = S // tm
    assert NA_V_PAD - NA_HEAD_DIM == MLA_V_PAD - MLA_V
    hq = MLA_HEADS * MLA_QK_PAD
    na_nq = NA_GROUP_ROWS * GRID_W
    weights = (w["g_mix"], w["w_kna"], w["w_qnaT"], w["w_vnaT"], w["w_zq"], w["w_zkv"], w["w_zrr"], w["w_g"],
               w["g_q"], w["w_uqT"], w["g_kv"], w["w_uk"], w["w_uvT"], w["place"])
    in_specs = [pl.BlockSpec((1, tm, D), lambda b, i: (b, i, 0))]
    in_specs += [_const_spec(a.shape) for a in weights]
    in_specs += [
        pl.BlockSpec((MLA_ROPE // 2, tm), lambda b, i: (0, i)),
        pl.BlockSpec((MLA_ROPE // 2, tm), lambda b, i: (0, i)),
        pl.BlockSpec((tm, MLA_ROPE), lambda b, i: (i, 0)),
        pl.BlockSpec((tm, MLA_ROPE), lambda b, i: (i, 0)),
    ]
    out_shape = (
        jax.ShapeDtypeStruct((B, S, NA_WIDTH), BF16),
        jax.ShapeDtypeStruct((B, S // na_nq, NA_WIDTH, na_nq), BF16),
        jax.ShapeDtypeStruct((B, S // NA_V_CHUNK, NA_HEADS * NA_V_PAD, NA_V_CHUNK), BF16),
        jax.ShapeDtypeStruct((B, S, 2 * D), F32),
        jax.ShapeDtypeStruct((B, S // MLA_TQ, hq, MLA_TQ), BF16),
        jax.ShapeDtypeStruct((B, S, hq), BF16),
        jax.ShapeDtypeStruct((B, S // MLA_TK, MLA_HEADS * MLA_V_PAD, MLA_TK), BF16),
    )
    out_specs = (
        pl.BlockSpec((1, tm, NA_WIDTH), lambda b, i: (b, i, 0)),
        _chunked_spec(NA_WIDTH, tm, na_nq),
        _chunked_spec(NA_HEADS * NA_V_PAD, tm, NA_V_CHUNK),
        pl.BlockSpec((1, tm, 2 * D), lambda b, i: (b, i, 0)),
        _chunked_spec(hq, tm, MLA_TQ),
        pl.BlockSpec((1, tm, hq), lambda b, i: (b, i, 0)),
        _chunked_spec(MLA_HEADS * MLA_V_PAD, tm, MLA_TK),
    )
    return pl.pallas_call(
        _inproj_body,
        grid=(B, nt),
        in_specs=in_specs,
        out_specs=out_specs,
        out_shape=out_shape,
        compiler_params=pltpu.CompilerParams(
            dimension_semantics=("arbitrary", "arbitrary"), vmem_limit_bytes=VMEM_LIMIT_BYTES),
        name="inproj",
    )(x, *weights, tabs["cosT"][:, :S], tabs["sinT"][:, :S], tabs["cos"][:S], tabs["sin"][:S])


def _na_body(qT_ref, k_ref, vT_ref, bias_ref, o_ref, *s_refs, rows, unroll):
    ng, nq = qT_ref.shape[1], qT_ref.shape[3]
    nk = NA_KEY_ROWS * GRID_W
    nbuf = len(s_refs)
    row = lax.broadcasted_iota(jnp.int32, (2 * NA_HEAD_DIM, nq), 0)

    def key_row0(g):
        return jnp.clip(NA_GROUP_ROWS * g - NA_WIN_H // 2, 0, rows - NA_KEY_ROWS)

    def scores_into(g, hh, s_ref):
        tok0 = pl.multiple_of(key_row0(g) * GRID_W, NA_V_CHUNK)
        kwin = k_ref[0, pl.ds(tok0, nk), :]
        qT = qT_ref[0, g]
        lo = hh * NA_HEAD_DIM
        q_h = jnp.where((row >= lo) & (row < lo + NA_HEAD_DIM), qT, jnp.zeros_like(qT))
        kind = jnp.where(g == 0, 0, jnp.where(g == ng - 1, 2, 1))
        s = _dot(kwin, q_h) + bias_ref[kind, hh]
        s_ref[...] = s
        return jnp.max(s, axis=0, keepdims=True)

    def softmax_pv(g, hh, s_ref, mx):
        c0 = lax.shift_right_logical(key_row0(g), 2)
        pb = jnp.exp2(s_ref[...] - mx).astype(BF16)
        acc = jnp.zeros((NA_V_PAD, nq), F32)
        for c in range(nk // NA_V_CHUNK):
            v_c = vT_ref[0, c0 + c, hh * NA_V_PAD:(hh + 1) * NA_V_PAD, :]
            acc = acc + _dot(v_c, pb[c * NA_V_CHUNK:(c + 1) * NA_V_CHUNK])
        o_ref[0, g, hh * NA_HEAD_DIM:(hh + 1) * NA_HEAD_DIM, :] = (
            acc[:NA_HEAD_DIM] / acc[NA_HEAD_DIM:NA_HEAD_DIM + 1]).astype(BF16)

    def trip(i, mxs):
        g0 = i * (unroll // 2)
        for u in range(unroll):
            ahead = u + NA_LOOKAHEAD
            g_ahead = jnp.minimum(g0 + ahead // 2, ng - 1)
            mx_new = scores_into(g_ahead, ahead % 2, s_refs[ahead % nbuf])
            softmax_pv(g0 + u // 2, u % 2, s_refs[u % nbuf], mxs[0])
            mxs = mxs[1:] + (mx_new,)
        return mxs

    mxs = tuple(scores_into(t // 2, t % 2, s_refs[t]) for t in range(NA_LOOKAHEAD))
    lax.fori_loop(0, 2 * ng // unroll, trip, mxs)


def _na(qT, k, vT, biasT):
    B, ng, _, nq = qT.shape
    S = k.shape[1]
    rows = S // GRID_W
    nk = NA_KEY_ROWS * GRID_W
    pair = 2 * NA_HEAD_DIM
    assert nq == NA_GROUP_ROWS * GRID_W and rows >= NA_KEY_ROWS
    unroll = next(u for u in range(NA_UNROLL, 0, -NA_SCORE_BUFS) if (2 * ng) % u == 0)
    assert NA_UNROLL % NA_SCORE_BUFS == 0 and NA_SCORE_BUFS % 2 == 0 and NA_LOOKAHEAD < NA_SCORE_BUFS <= unroll
    return pl.pallas_call(
        functools.partial(_na_body, rows=rows, unroll=unroll),
        grid=(B, NA_HEADS // 2),
        in_specs=[
            pl.BlockSpec((1, ng, pair, nq), lambda b, hp: (b, 0, hp, 0)),
            pl.BlockSpec((1, S, pair), lambda b, hp: (b, 0, hp)),
            pl.BlockSpec((1, S // NA_V_CHUNK, 2 * NA_V_PAD, NA_V_CHUNK), lambda b, hp: (b, 0, hp, 0)),
            pl.BlockSpec((3, 2, nk, nq), lambda b, hp: (0, hp, 0, 0)),
        ],
        out_specs=pl.BlockSpec((1, ng, pair, nq), lambda b, hp: (b, 0, hp, 0)),
        out_shape=jax.ShapeDtypeStruct((B, ng, NA_WIDTH, nq), BF16),
        scratch_shapes=[pltpu.VMEM((nk, nq), F32)] * NA_SCORE_BUFS,
        compiler_params=pltpu.CompilerParams(
            dimension_semantics=("arbitrary", "arbitrary"), vmem_limit_bytes=VMEM_LIMIT_BYTES),
        name="natten",
    )(qT, k, vT, biasT)


def _na_bias_table(rpb):
    n_off_r, n_off_c = 2 * NA_WIN_H - 1, 2 * NA_WIN_W - 1
    cols = np.arange(GRID_W)
    cs = np.clip(cols - NA_WIN_W // 2, 0, GRID_W - NA_WIN_W)
    col_ok = (cols[:, None] >= cs[None, :]) & (cols[:, None] < cs[None, :] + NA_WIN_W)
    col_idx = np.clip(cols[:, None] - cols[None, :] + NA_WIN_W - 1, 0, n_off_c - 1)
    onehot = (col_idx[None] == np.arange(n_off_c)[:, None, None]) & col_ok[None]
    onehot = jnp.asarray(onehot.reshape(n_off_c, GRID_W * GRID_W), F32)
    blocks = jnp.dot(rpb.reshape(NA_HEADS * n_off_r, n_off_c), onehot, precision=lax.Precision.HIGHEST)
    blocks = blocks.reshape(NA_HEADS, n_off_r, GRID_W, GRID_W)
    blocks = jnp.where(jnp.asarray(col_ok)[None, None], blocks * LOG2_E, MASK_VALUE)
    masked = jnp.full((NA_HEADS, 1, GRID_W, GRID_W), MASK_VALUE, F32)
    blocks = jnp.concatenate([blocks, masked], axis=1)

    rows = 4 * NA_KEY_ROWS
    kinds = ((0, 0), (NA_KEY_ROWS, NA_KEY_ROWS - NA_WIN_H // 2), (rows - NA_GROUP_ROWS, rows - NA_KEY_ROWS))
    sel = np.zeros((3, NA_KEY_ROWS, NA_GROUP_ROWS), np.int32)
    for t, (r0, ws) in enumerate(kinds):
        q_row = r0 + np.arange(NA_GROUP_ROWS)
        k_row = ws + np.arange(NA_KEY_ROWS)
        rs = np.clip(q_row - NA_WIN_H // 2, 0, rows - NA_WIN_H)
        row_ok = (k_row[:, None] >= rs[None, :]) & (k_row[:, None] < rs[None, :] + NA_WIN_H)
        sel[t] = np.where(row_ok, k_row[:, None] - q_row[None, :] + NA_WIN_H - 1, n_off_r)
    table = jnp.take(blocks, jnp.asarray(sel.reshape(-1)), axis=1)
    table = table.reshape(NA_HEADS, 3, NA_KEY_ROWS, NA_GROUP_ROWS, GRID_W, GRID_W)
    table = table.transpose(1, 0, 2, 4, 3, 5)
    return table.reshape(3, NA_HEADS, NA_KEY_ROWS * GRID_W, NA_GROUP_ROWS * GRID_W)


def _mla_body(qT_ref, k_ref, vT_ref, o_ref, *s_refs, unroll):
    nq, tq = qT_ref.shape[1], qT_ref.shape[3]
    nchunk, tk = vT_ref.shape[1], vT_ref.shape[3]
    nbuf = len(s_refs)
    nsteps = nq * nchunk

    def scores_into(qb, j, s_ref):
        s = _dot(k_ref[0, pl.ds(pl.multiple_of(j * tk, tk), tk), :], qT_ref[0, qb])
        s_ref[...] = s
        return jnp.max(s, axis=0, keepdims=True)

    def softmax_pv(j, s_ref, mx, m, acc):
        m_new = jnp.maximum(m, mx)
        alpha = jnp.exp2(m - m_new)
        p = jnp.exp2(s_ref[...] - m_new).astype(BF16)
        return m_new, alpha * acc + _dot(vT_ref[0, j], p)

    def trip(i, carry):
        m, acc, mxs = carry
        t0 = i * unroll
        qb = lax.div(t0, nchunk)
        j0 = t0 - qb * nchunk
        t1 = jnp.minimum(t0 + unroll, nsteps - unroll)
        qb1 = lax.div(t1, nchunk)
        j1 = t1 - qb1 * nchunk
        for u in range(unroll):
            ahead = u + MLA_LOOKAHEAD
            if ahead < unroll:
                mx_new = scores_into(qb, j0 + ahead, s_refs[ahead % nbuf])
            else:
                mx_new = scores_into(qb1, j1 + ahead - unroll, s_refs[ahead % nbuf])
            m, acc = softmax_pv(j0 + u, s_refs[u % nbuf], mxs[0], m, acc)
            mxs = mxs[1:] + (mx_new,)
        done = j0 + unroll == nchunk

        @pl.when(done)
        def _():
            o_ref[0, qb] = (acc[:MLA_V] / acc[MLA_V:MLA_V + 1]).astype(BF16)

        m = jnp.where(done, -jnp.inf, m)
        acc = jnp.where(done, 0.0, acc)
        return m, acc, mxs

    mxs = tuple(scores_into(0, t, s_refs[t]) for t in range(MLA_LOOKAHEAD))
    init = (jnp.full((1, tq), -jnp.inf, F32), jnp.zeros((MLA_V_PAD, tq), F32), mxs)
    lax.fori_loop(0, nsteps // unroll, trip, init)


def _mla(qT, k, vT):
    B, nq, _, tq = qT.shape
    S = k.shape[1]
    nchunk, tk = vT.shape[1], vT.shape[3]
    unroll = next(u for u in range(MLA_UNROLL, 0, -MLA_SCORE_BUFS) if nchunk % u == 0)
    assert MLA_UNROLL % MLA_SCORE_BUFS == 0 and MLA_LOOKAHEAD < MLA_SCORE_BUFS <= unroll
    return pl.pallas_call(
        functools.partial(_mla_body, unroll=unroll),
        grid=(B, MLA_HEADS),
        in_specs=[
            pl.BlockSpec((1, nq, MLA_QK_PAD, tq), lambda b, h: (b, 0, h, 0)),
            pl.BlockSpec((1, S, MLA_QK_PAD), lambda b, h: (b, 0, h)),
            pl.BlockSpec((1, nchunk, MLA_V_PAD, tk), lambda b, h: (b, 0, h, 0)),
        ],
        out_specs=pl.BlockSpec((1, nq, MLA_V, tq), lambda b, h: (b, 0, h, 0)),
        out_shape=jax.ShapeDtypeStruct((B, nq, MLA_WIDTH, tq), BF16),
        scratch_shapes=[pltpu.VMEM((tk, tq), F32)] * MLA_SCORE_BUFS,
        compiler_params=pltpu.CompilerParams(
            dimension_semantics=("arbitrary", "arbitrary"), vmem_limit_bytes=VMEM_LIMIT_BYTES),
        name="mla",
    )(qT, k, vT)


def _post_body(x_ref, oTna_ref, oTmla_ref, gates_ref, p_ref,
               wnao_ref, wmlao_ref, wout_ref, gffn_ref, wgate_ref, wup_ref, wdown_ref,
               gple_ref, wpleg_ref, wple_ref, gfin_ref, y_ref):
    d = x_ref.shape[2]
    oTna = jnp.concatenate([oTna_ref[0, c] for c in range(oTna_ref.shape[1])], axis=1)
    a = _dot_tn(oTna, wnao_ref[...])
    oTmla = jnp.concatenate([oTmla_ref[0, c] for c in range(oTmla_ref.shape[1])], axis=1)
    b = _dot_tn(oTmla, wmlao_ref[...])
    gates = gates_ref[0]
    mix = (gates[:, :d] * a + gates[:, d:] * b).astype(BF16)
    x1 = x_ref[0] + _dot(mix, wout_ref[...])

    hn = _rms(x1, gffn_ref[...]).astype(BF16)
    gate = _dot(hn, wgate_ref[...])
    up = _dot(hn, wup_ref[...])
    act = (gate * jax.nn.sigmoid(gate) * up).astype(BF16)
    x2 = x1 + _dot(act, wdown_ref[...])

    ple_gate = jax.nn.sigmoid(_dot(_rms(x2, gple_ref[...]).astype(BF16), wpleg_ref[...]))
    x3 = x2 + ple_gate * _dot(p_ref[0].astype(BF16), wple_ref[...])
    y_ref[0] = _rms(x3, gfin_ref[...])


def _post(x, oTna, oTmla, gates, p, w):
    B, S, D = x.shape
    tm = TM_POST
    weights = (w["w_na_o"], w["w_mla_o"], w["w_out"], w["g_ffn"], w["w_gate"], w["w_up"], w["w_down"],
               w["g_ple"], w["w_ple_gate"], w["w_ple"], w["g_final"])
    in_specs = [
        pl.BlockSpec((1, tm, D), lambda b, i: (b, i, 0)),
        _chunked_spec(NA_WIDTH, tm, NA_GROUP_ROWS * GRID_W),
        _chunked_spec(MLA_WIDTH, tm, MLA_TQ),
        pl.BlockSpec((1, tm, 2 * D), lambda b, i: (b, i, 0)),
        pl.BlockSpec((1, tm, PLE_DIM), lambda b, i: (b, i, 0)),
    ]
    in_specs += [_const_spec(a.shape) for a in weights]
    return pl.pallas_call(
        _post_body,
        grid=(B, S // tm),
        in_specs=in_specs,
        out_specs=pl.BlockSpec((1, tm, D), lambda b, i: (b, i, 0)),
        out_shape=jax.ShapeDtypeStruct((B, S, D), F32),
        compiler_params=pltpu.CompilerParams(
            dimension_semantics=("arbitrary", "arbitrary"), vmem_limit_bytes=VMEM_LIMIT_BYTES),
        name="post",
    )(x, oTna, oTmla, gates, p, *weights)


def _prep_weights(g_mix, w_in, g_q_lat, w_uq, g_kv_lat, w_ukv, w_na_o, w_mla_o, w_out,
                  g_ffn, w_gate_up, w_down, g_ple, w_ple_gate, w_ple, g_final):
    o = np.cumsum((0, NA_WIDTH, NA_WIDTH, NA_WIDTH, Q_LORA, KV_LORA, MLA_ROPE, 2 * D_MODEL))
    w_q, w_k, w_v, w_zq, w_zkv, w_zr, w_g = (w_in[:, o[i]:o[i + 1]] for i in range(7))
    half = MLA_ROPE // 2
    w_zr_rot = jnp.concatenate([-w_zr[:, half:], w_zr[:, :half]], axis=1)

    w_uq_h = w_uq.reshape(Q_LORA, MLA_HEADS, MLA_NOPE + MLA_ROPE)
    w_uq_pad = jnp.pad(w_uq_h, ((0, 0), (0, 0), (0, MLA_QK_PAD - MLA_NOPE - MLA_ROPE)))
    w_uqT = w_uq_pad.reshape(Q_LORA, MLA_HEADS * MLA_QK_PAD).T

    w_ukv_h = w_ukv.reshape(KV_LORA, MLA_HEADS, MLA_NOPE + MLA_V)
    w_uk = jnp.pad(w_ukv_h[:, :, :MLA_NOPE], ((0, 0), (0, 0), (0, MLA_QK_PAD - MLA_NOPE)))
    w_uk = w_uk.reshape(KV_LORA, MLA_HEADS * MLA_QK_PAD)
    w_uvT = w_ukv_h[:, :, MLA_NOPE:].reshape(KV_LORA, MLA_WIDTH).T

    place = np.zeros((MLA_ROPE, MLA_HEADS, MLA_QK_PAD), np.float32)
    for j in range(MLA_ROPE):
        place[j, :, MLA_NOPE + j] = 1.0
    place = place.reshape(MLA_ROPE, MLA_HEADS * MLA_QK_PAD)

    w_gate = w_gate_up[:, :D_FF]
    w_up = w_gate_up[:, D_FF:]
    w_dn = w_down

    row = lambda g: g.reshape(1, -1).astype(F32)
    return {
        "g_mix": row(g_mix),
        "w_kna": w_k.astype(BF16),
        "w_qnaT": (w_q * NA_HEAD_DIM ** -0.5).T.astype(BF16),
        "w_vnaT": w_v.T.astype(BF16),
        "w_zq": w_zq.astype(BF16),
        "w_zkv": w_zkv.astype(BF16),
        "w_zrr": jnp.concatenate([w_zr, w_zr_rot], axis=1).astype(BF16),
        "w_g": w_g.astype(BF16),
        "g_q": row(g_q_lat),
        "w_uqT": w_uqT.astype(BF16),
        "g_kv": row(g_kv_lat),
        "w_uk": w_uk.astype(BF16),
        "w_uvT": w_uvT.astype(BF16),
        "place": jnp.asarray(place, BF16),
        "w_na_o": w_na_o.astype(BF16),
        "w_mla_o": w_mla_o.astype(BF16),
        "w_out": w_out.astype(BF16),
        "g_ffn": row(g_ffn),
        "w_gate": w_gate.astype(BF16),
        "w_up": w_up.astype(BF16),
        "w_down": w_dn.astype(BF16),
        "g_ple": row(g_ple),
        "w_ple_gate": w_ple_gate.astype(BF16),
        "w_ple": w_ple.astype(BF16),
        "g_final": row(g_final),
    }


def _rope_tables(s_max):
    half = MLA_ROPE // 2
    pos = jnp.arange(s_max, dtype=F32)
    freqs = ROPE_THETA ** (-jnp.arange(half, dtype=F32) / half)
    ang = pos[:, None] * freqs[None, :]
    cos, sin = jnp.cos(ang), jnp.sin(ang)
    return {"cosT": cos.T, "sinT": sin.T,
            "cos": jnp.concatenate([cos, cos], axis=1), "sin": jnp.concatenate([sin, sin], axis=1)}


def _trunk(x, p, w, tabs, biasT):
    kna, qTna, vTna, gates, qT, k, vT = _inproj(x, w, tabs)
    oTna = _na(qTna, kna, vTna, biasT)
    oTmla = _mla(qT, k, vT)
    return _post(x, oTna, oTmla, gates, p, w)


def kernel(x_prompt, x_sample, p_prompt, p_sample, g_mix, w_in, rpb, g_q_lat, w_uq, g_kv_lat, w_ukv,
           w_na_o, w_mla_o, w_out, g_ffn, w_gate_up, w_down, g_ple, w_ple_gate, w_ple, g_final):
    assert g_mix.shape[0] == 1, "single-layer trunk"
    w = _prep_weights(g_mix[0], w_in[0], g_q_lat[0], w_uq[0], g_kv_lat[0], w_ukv[0], w_na_o[0], w_mla_o[0],
                      w_out[0], g_ffn[0], w_gate_up[0], w_down[0], g_ple[0], w_ple_gate[0], w_ple[0], g_final)
    biasT = _na_bias_table(rpb[0])
    tabs = _rope_tables(max(x_prompt.shape[1], x_sample.shape[1]))
    y_prompt = _trunk(x_prompt, p_prompt[0], w, tabs, biasT)
    y_sample = _trunk(x_sample, p_sample[0], w, tabs, biasT)
    return (y_prompt, y_sample)
```

```python
import functools

import numpy as np
import jax
import jax.numpy as jnp
from jax import lax
from jax.experimental import pallas as pl
from jax.experimental.pallas import tpu as pltpu

D_MODEL = 1024
GRID_W = 64
NA_HEADS = 8
NA_HEAD_DIM = 64
NA_WIN_H = 8
NA_WIN_W = 16
MLA_HEADS = 8
MLA_NOPE = 64
MLA_ROPE = 32
MLA_V = 64
Q_LORA = 384
KV_LORA = 256
ROPE_THETA = 10000.0
D_FF = 2816
PLE_DIM = 256
EPS = 1e-6
NA_WIDTH = NA_HEADS * NA_HEAD_DIM
MLA_WIDTH = MLA_HEADS * MLA_V

MLA_QK_PAD = 128
MLA_V_PAD = 80
NA_V_PAD = 80
LOG2_E = 1.4426950408889634
NA_GROUP_ROWS = 4
NA_KEY_ROWS = 12
NA_UNROLL = 32
NA_SCORE_BUFS = 4
NA_LOOKAHEAD = 2
NA_V_CHUNK = 256
MASK_VALUE = -1e30

TM_IN = 256
TM_POST = 256
MLA_TQ = 512
MLA_TK = 256
MLA_UNROLL = 64
MLA_SCORE_BUFS = 4
MLA_LOOKAHEAD = 2
VMEM_LIMIT_BYTES = 56 * 1024 * 1024

BF16 = jnp.bfloat16
F32 = jnp.float32


def _dot(a, b):
    return jnp.dot(a, b, preferred_element_type=F32)


def _dot_nt(a, b):
    return lax.dot_general(a, b, (((1,), (1,)), ((), ())), preferred_element_type=F32)


def _dot_tn(a, b):
    return lax.dot_general(a, b, (((0,), (0,)), ((), ())), preferred_element_type=F32)


def _rms(x, g):
    ms = jnp.mean(x * x, axis=-1, keepdims=True)
    return x * lax.rsqrt(ms + EPS) * g


def _const_spec(shape):
    nd = len(shape)
    return pl.BlockSpec(shape, lambda *_: (0,) * nd, pipeline_mode=pl.Buffered(1))


def _chunked_spec(rows, tm, chunk):
    if tm >= chunk:
        return pl.BlockSpec((1, tm // chunk, rows, chunk), lambda b, i: (b, i, 0, 0))
    sub = chunk // tm
    return pl.BlockSpec((1, 1, rows, tm), lambda b, i: (b, i // sub, 0, i % sub))


def _store_chunked(ref, r0, val):
    width = ref.shape[3]
    for c in range(ref.shape[1]):
        ref[0, c, r0:r0 + val.shape[0], :] = val[:, c * width:(c + 1) * width]


def _inproj_body(x_ref, gmix_ref, wkna_ref, wqnaT_ref, wvnaT_ref, wzq_ref, wzkv_ref, wzrr_ref, wg_ref,
                 gq_ref, wuqT_ref, gkv_ref, wuk_ref, wuvT_ref, place_ref,
                 cosT_ref, sinT_ref, cos_ref, sin_ref,
                 kna_ref, qTna_ref, vTna_ref, gates_ref, qT_ref, k_ref, vT_ref):
    tm = x_ref.shape[1]
    h = _rms(x_ref[0], gmix_ref[...]).astype(BF16)

    zqn = _rms(_dot(h, wzq_ref[...]), gq_ref[...]).astype(BF16)
    zkvn = _rms(_dot(h, wzkv_ref[...]), gkv_ref[...]).astype(BF16)
    zrr = _dot(h, wzrr_ref[...])

    gates_ref[0] = jax.nn.sigmoid(_dot(h, wg_ref[...]))

    kna_ref[0] = _dot(h, wkna_ref[...]).astype(BF16)
    _store_chunked(qTna_ref, 0, (_dot_nt(wqnaT_ref[...], h) * LOG2_E).astype(BF16))
    vTna = _dot_nt(wvnaT_ref[...], h).astype(BF16)
    ones_rows = (lax.broadcasted_iota(jnp.int32, (NA_V_PAD - NA_HEAD_DIM, tm), 0) == 0).astype(F32).astype(BF16)
    for hh in range(NA_HEADS):
        _store_chunked(vTna_ref, hh * NA_V_PAD, vTna[hh * NA_HEAD_DIM:(hh + 1) * NA_HEAD_DIM])
        _store_chunked(vTna_ref, hh * NA_V_PAD + NA_HEAD_DIM, ones_rows)

    scale = (MLA_NOPE + MLA_ROPE) ** -0.5 * LOG2_E
    qT = _dot_nt(wuqT_ref[...], zqn)
    cT = cosT_ref[...]
    sT = sinT_ref[...]
    half = MLA_ROPE // 2
    for hh in range(MLA_HEADS):
        base = hh * MLA_QK_PAD
        nope = qT[base:base + MLA_NOPE]
        x1 = qT[base + MLA_NOPE:base + MLA_NOPE + half]
        x2 = qT[base + MLA_NOPE + half:base + MLA_NOPE + MLA_ROPE]
        _store_chunked(qT_ref, base, (nope * scale).astype(BF16))
        _store_chunked(qT_ref, base + MLA_NOPE, ((x1 * cT - x2 * sT) * scale).astype(BF16))
        _store_chunked(qT_ref, base + MLA_NOPE + half, ((x2 * cT + x1 * sT) * scale).astype(BF16))
        _store_chunked(qT_ref, base + MLA_NOPE + MLA_ROPE, jnp.zeros((MLA_QK_PAD - MLA_NOPE - MLA_ROPE, tm), BF16))

    k_rope = zrr[:, :MLA_ROPE] * cos_ref[...] + zrr[:, MLA_ROPE:] * sin_ref[...]
    k_nope = _dot(zkvn, wuk_ref[...])
    k_ref[0] = (k_nope + _dot(k_rope.astype(BF16), place_ref[...])).astype(BF16)

    vT = _dot_nt(wuvT_ref[...], zkvn).astype(BF16)
    for hh in range(MLA_HEADS):
        _store_chunked(vT_ref, hh * MLA_V_PAD, vT[hh * MLA_V:(hh + 1) * MLA_V])
        _store_chunked(vT_ref, hh * MLA_V_PAD + MLA_V, ones_rows)


def _inproj(x, w, tabs):
    B, S, D = x.shape
    tm = TM_IN
    nt = S // tm
    assert NA_V_PAD - NA_HEAD_DIM == MLA_V_PAD - MLA_V
    hq = MLA_HEADS * MLA_QK_PAD
    na_nq = NA_GROUP_ROWS * GRID_W
    weights = (w["g_mix"], w["w_kna"], w["w_qnaT"], w["w_vnaT"], w["w_zq"], w["w_zkv"], w["w_zrr"], w["w_g"],
               w["g_q"], w["w_uqT"], w["g_kv"], w["w_uk"], w["w_uvT"], w["place"])
    in_specs = [pl.BlockSpec((1, tm, D), lambda b, i: (b, i, 0))]
    in_specs += [_const_spec(a.shape) for a in weights]
    in_specs += [
        pl.BlockSpec((MLA_ROPE // 2, tm), lambda b, i: (0, i)),
        pl.BlockSpec((MLA_ROPE // 2, tm), lambda b, i: (0, i)),
        pl.BlockSpec((tm, MLA_ROPE), lambda b, i: (i, 0)),
        pl.BlockSpec((tm, MLA_ROPE), lambda b, i: (i, 0)),
    ]
    out_shape = (
        jax.ShapeDtypeStruct((B, S, NA_WIDTH), BF16),
        jax.ShapeDtypeStruct((B, S // na_nq, NA_WIDTH, na_nq), BF16),
        jax.ShapeDtypeStruct((B, S // NA_V_CHUNK, NA_HEADS * NA_V_PAD, NA_V_CHUNK), BF16),
        jax.ShapeDtypeStruct((B, S, 2 * D), F32),
        jax.ShapeDtypeStruct((B, S // MLA_TQ, hq, MLA_TQ), BF16),
        jax.ShapeDtypeStruct((B, S, hq), BF16),
        jax.ShapeDtypeStruct((B, S // MLA_TK, MLA_HEADS * MLA_V_PAD, MLA_TK), BF16),
    )
    out_specs = (
        pl.BlockSpec((1, tm, NA_WIDTH), lambda b, i: (b, i, 0)),
        _chunked_spec(NA_WIDTH, tm, na_nq),
        _chunked_spec(NA_HEADS * NA_V_PAD, tm, NA_V_CHUNK),
        pl.BlockSpec((1, tm, 2 * D), lambda b, i: (b, i, 0)),
        _chunked_spec(hq, tm, MLA_TQ),
        pl.BlockSpec((1, tm, hq), lambda b, i: (b, i, 0)),
        _chunked_spec(MLA_HEADS * MLA_V_PAD, tm, MLA_TK),
    )
    return pl.pallas_call(
        _inproj_body,
        grid=(B, nt),
        in_specs=in_specs,
        out_specs=out_specs,
        out_shape=out_shape,
        compiler_params=pltpu.CompilerParams(
            dimension_semantics=("arbitrary", "arbitrary"), vmem_limit_bytes=VMEM_LIMIT_BYTES),
        name="inproj",
    )(x, *weights, tabs["cosT"][:, :S], tabs["sinT"][:, :S], tabs["cos"][:S], tabs["sin"][:S])


def _na_body(qT_ref, k_ref, vT_ref, bias_ref, o_ref, *s_refs, rows, unroll):
    ng, nq = qT_ref.shape[1], qT_ref.shape[3]
    nk = NA_KEY_ROWS * GRID_W
    nbuf = len(s_refs)
    row = lax.broadcasted_iota(jnp.int32, (2 * NA_HEAD_DIM, nq), 0)

    def key_row0(g):
        return jnp.clip(NA_GROUP_ROWS * g - NA_WIN_H // 2, 0, rows - NA_KEY_ROWS)

    def scores_into(g, hh, s_ref):
        tok0 = pl.multiple_of(key_row0(g) * GRID_W, NA_V_CHUNK)
        kwin = k_ref[0, pl.ds(tok0, nk), :]
        qT = qT_ref[0, g]
        lo = hh * NA_HEAD_DIM
        q_h = jnp.where((row >= lo) & (row < lo + NA_HEAD_DIM), qT, jnp.zeros_like(qT))
        kind = jnp.where(g == 0, 0, jnp.where(g == ng - 1, 2, 1))
        s = _dot(kwin, q_h) + bias_ref[kind, hh]
        s_ref[...] = s
        return jnp.max(s, axis=0, keepdims=True)

    def softmax_pv(g, hh, s_ref, mx):
        c0 = lax.shift_right_logical(key_row0(g), 2)
        pb = jnp.exp2(s_ref[...] - mx).astype(BF16)
        acc = jnp.zeros((NA_V_PAD, nq), F32)
        for c in range(nk // NA_V_CHUNK):
            v_c = vT_ref[0, c0 + c, hh * NA_V_PAD:(hh + 1) * NA_V_PAD, :]
            acc = acc + _dot(v_c, pb[c * NA_V_CHUNK:(c + 1) * NA_V_CHUNK])
        o_ref[0, g, hh * NA_HEAD_DIM:(hh + 1) * NA_HEAD_DIM, :] = (
            acc[:NA_HEAD_DIM] / acc[NA_HEAD_DIM:NA_HEAD_DIM + 1]).astype(BF16)

    def trip(i, mxs):
        g0 = i * (unroll // 2)
        for u in range(unroll):
            ahead = u + NA_LOOKAHEAD
            g_ahead = jnp.minimum(g0 + ahead // 2, ng - 1)
            mx_new = scores_into(g_ahead, ahead % 2, s_refs[ahead % nbuf])
            softmax_pv(g0 + u // 2, u % 2, s_refs[u % nbuf], mxs[0])
            mxs = mxs[1:] + (mx_new,)
        return mxs

    mxs = tuple(scores_into(t // 2, t % 2, s_refs[t]) for t in range(NA_LOOKAHEAD))
    lax.fori_loop(0, 2 * ng // unroll, trip, mxs)


def _na(qT, k, vT, biasT):
    B, ng, _, nq = qT.shape
    S = k.shape[1]
    rows = S // GRID_W
    nk = NA_KEY_ROWS * GRID_W
    pair = 2 * NA_HEAD_DIM
    assert nq == NA_GROUP_ROWS * GRID_W and rows >= NA_KEY_ROWS
    unroll = next(u for u in range(NA_UNROLL, 0, -NA_SCORE_BUFS) if (2 * ng) % u == 0)
    assert NA_UNROLL % NA_SCORE_BUFS == 0 and NA_SCORE_BUFS % 2 == 0 and NA_LOOKAHEAD < NA_SCORE_BUFS <= unroll
    return pl.pallas_call(
        functools.partial(_na_body, rows=rows, unroll=unroll),
        grid=(B, NA_HEADS // 2),
        in_specs=[
            pl.BlockSpec((1, ng, pair, nq), lambda b, hp: (b, 0, hp, 0)),
            pl.BlockSpec((1, S, pair), lambda b, hp: (b, 0, hp)),
            pl.BlockSpec((1, S // NA_V_CHUNK, 2 * NA_V_PAD, NA_V_CHUNK), lambda b, hp: (b, 0, hp, 0)),
            pl.BlockSpec((3, 2, nk, nq), lambda b, hp: (0, hp, 0, 0)),
        ],
        out_specs=pl.BlockSpec((1, ng, pair, nq), lambda b, hp: (b, 0, hp, 0)),
        out_shape=jax.ShapeDtypeStruct((B, ng, NA_WIDTH, nq), BF16),
        scratch_shapes=[pltpu.VMEM((nk, nq), F32)] * NA_SCORE_BUFS,
        compiler_params=pltpu.CompilerParams(
            dimension_semantics=("arbitrary", "arbitrary"), vmem_limit_bytes=VMEM_LIMIT_BYTES),
        name="natten",
    )(qT, k, vT, biasT)


def _na_bias_table(rpb):
    n_off_r, n_off_c = 2 * NA_WIN_H - 1, 2 * NA_WIN_W - 1
    cols = np.arange(GRID_W)
    cs = np.clip(cols - NA_WIN_W // 2, 0, GRID_W - NA_WIN_W)
    col_ok = (cols[:, None] >= cs[None, :]) & (cols[:, None] < cs[None, :] + NA_WIN_W)
    col_idx = np.clip(cols[:, None] - cols[None, :] + NA_WIN_W - 1, 0, n_off_c - 1)
    onehot = (col_idx[None] == np.arange(n_off_c)[:, None, None]) & col_ok[None]
    onehot = jnp.asarray(onehot.reshape(n_off_c, GRID_W * GRID_W), F32)
    blocks = jnp.dot(rpb.reshape(NA_HEADS * n_off_r, n_off_c), onehot, precision=lax.Precision.HIGHEST)
    blocks = blocks.reshape(NA_HEADS, n_off_r, GRID_W, GRID_W)
    blocks = jnp.where(jnp.asarray(col_ok)[None, None], blocks * LOG2_E, MASK_VALUE)
    masked = jnp.full((NA_HEADS, 1, GRID_W, GRID_W), MASK_VALUE, F32)
    blocks = jnp.concatenate([blocks, masked], axis=1)

    rows = 4 * NA_KEY_ROWS
    kinds = ((0, 0), (NA_KEY_ROWS, NA_KEY_ROWS - NA_WIN_H // 2), (rows - NA_GROUP_ROWS, rows - NA_KEY_ROWS))
    sel = np.zeros((3, NA_KEY_ROWS, NA_GROUP_ROWS), np.int32)
    for t, (r0, ws) in enumerate(kinds):
        q_row = r0 + np.arange(NA_GROUP_ROWS)
        k_row = ws + np.arange(NA_KEY_ROWS)
        rs = np.clip(q_row - NA_WIN_H // 2, 0, rows - NA_WIN_H)
        row_ok = (k_row[:, None] >= rs[None, :]) & (k_row[:, None] < rs[None, :] + NA_WIN_H)
        sel[t] = np.where(row_ok, k_row[:, None] - q_row[None, :] + NA_WIN_H - 1, n_off_r)
    table = jnp.take(blocks, jnp.asarray(sel.reshape(-1)), axis=1)
    table = table.reshape(NA_HEADS, 3, NA_KEY_ROWS, NA_GROUP_ROWS, GRID_W, GRID_W)
    table = table.transpose(1, 0, 2, 4, 3, 5)
    return table.reshape(3, NA_HEADS, NA_KEY_ROWS * GRID_W, NA_GROUP_ROWS * GRID_W)


def _mla_body(qT_ref, k_ref, vT_ref, o_ref, *s_refs, unroll):
    nq, tq = qT_ref.shape[1], qT_ref.shape[3]
    nchunk, tk = vT_ref.shape[1], vT_ref.shape[3]
    nbuf = len(s_refs)

    def scores_into(qb, j, s_ref):
        s = _dot(k_ref[0, j * tk:(j + 1) * tk, :], qT_ref[0, qb])
        s_ref[...] = s
        return jnp.max(s, axis=0, keepdims=True)

    def softmax_pv(j, s_ref, mx, m, acc):
        m_new = jnp.maximum(m, mx)
        alpha = jnp.exp2(m - m_new)
        p = jnp.exp2(s_ref[...] - m_new).astype(BF16)
        return m_new, alpha * acc + _dot(vT_ref[0, j], p)

    blocks_per_trip = unroll // nchunk

    def trip(i, mxs):
        qb0 = i * blocks_per_trip
        qb1 = jnp.minimum(qb0 + blocks_per_trip, nq - blocks_per_trip)
        for u in range(unroll):
            if u % nchunk == 0:
                m, acc = jnp.full((1, tq), -jnp.inf, F32), jnp.zeros((MLA_V_PAD, tq), F32)
            ahead = u + MLA_LOOKAHEAD
            qb_ahead = (qb0 if ahead < unroll else qb1) + (ahead % unroll) // nchunk
            mx_new = scores_into(qb_ahead, ahead % nchunk, s_refs[ahead % nbuf])
            m, acc = softmax_pv(u % nchunk, s_refs[u % nbuf], mxs[0], m, acc)
            mxs = mxs[1:] + (mx_new,)
            if (u + 1) % nchunk == 0:
                o_ref[0, qb0 + u // nchunk] = (acc[:MLA_V] / acc[MLA_V:MLA_V + 1]).astype(BF16)
        return mxs

    mxs = tuple(scores_into(0, t, s_refs[t]) for t in range(MLA_LOOKAHEAD))
    lax.fori_loop(0, nq // blocks_per_trip, trip, mxs)


def _mla(qT, k, vT):
    B, nq, _, tq = qT.shape
    S = k.shape[1]
    nchunk, tk = vT.shape[1], vT.shape[3]
    unroll = next(u for u in range(MLA_UNROLL, 0, -MLA_SCORE_BUFS)
                  if u % nchunk == 0 and nq % (u // nchunk) == 0)
    assert MLA_UNROLL % MLA_SCORE_BUFS == 0 and MLA_LOOKAHEAD < MLA_SCORE_BUFS <= unroll
    return pl.pallas_call(
        functools.partial(_mla_body, unroll=unroll),
        grid=(B, MLA_HEADS),
        in_specs=[
            pl.BlockSpec((1, nq, MLA_QK_PAD, tq), lambda b, h: (b, 0, h, 0)),
            pl.BlockSpec((1, S, MLA_QK_PAD), lambda b, h: (b, 0, h)),
            pl.BlockSpec((1, nchunk, MLA_V_PAD, tk), lambda b, h: (b, 0, h, 0)),
        ],
        out_specs=pl.BlockSpec((1, nq, MLA_V, tq), lambda b, h: (b, 0, h, 0)),
        out_shape=jax.ShapeDtypeStruct((B, nq, MLA_WIDTH, tq), BF16),
        scratch_shapes=[pltpu.VMEM((tk, tq), F32)] * MLA_SCORE_BUFS,
        compiler_params=pltpu.CompilerParams(
            dimension_semantics=("arbitrary", "arbitrary"), vmem_limit_bytes=VMEM_LIMIT_BYTES),
        name="mla",
    )(qT, k, vT)


def _post_body(x_ref, oTna_ref, oTmla_ref, gates_ref, p_ref,
               wnao_ref, wmlao_ref, wout_ref, gffn_ref, wgate_ref, wup_ref, wdown_ref,
               gple_ref, wpleg_ref, wple_ref, gfin_ref, y_ref):
    d = x_ref.shape[2]
    oTna = jnp.concatenate([oTna_ref[0, c] for c in range(oTna_ref.shape[1])], axis=1)
    a = _dot_tn(oTna, wnao_ref[...])
    oTmla = jnp.concatenate([oTmla_ref[0, c] for c in range(oTmla_ref.shape[1])], axis=1)
    b = _dot_tn(oTmla, wmlao_ref[...])
    gates = gates_ref[0]
    mix = (gates[:, :d] * a + gates[:, d:] * b).astype(BF16)
    x1 = x_ref[0] + _dot(mix, wout_ref[...])
    ple = _dot(p_ref[0].astype(BF16), wple_ref[...])

    hn = _rms(x1, gffn_ref[...]).astype(BF16)
    gate = _dot(hn, wgate_ref[...])
    up = _dot(hn, wup_ref[...])
    act = (gate * jax.nn.sigmoid(gate) * up).astype(BF16)
    x2 = x1 + _dot(act, wdown_ref[...])

    ple_gate = jax.nn.sigmoid(_dot(_rms(x2, gple_ref[...]).astype(BF16), wpleg_ref[...]))
    x3 = x2 + ple_gate * ple
    y_ref[0] = _rms(x3, gfin_ref[...])


def _post(x, oTna, oTmla, gates, p, w):
    B, S, D = x.shape
    tm = TM_POST
    weights = (w["w_na_o"], w["w_mla_o"], w["w_out"], w["g_ffn"], w["w_gate"], w["w_up"], w["w_down"],
               w["g_ple"], w["w_ple_gate"], w["w_ple"], w["g_final"])
    in_specs = [
        pl.BlockSpec((1, tm, D), lambda b, i: (b, i, 0)),
        _chunked_spec(NA_WIDTH, tm, NA_GROUP_ROWS * GRID_W),
        _chunked_spec(MLA_WIDTH, tm, MLA_TQ),
        pl.BlockSpec((1, tm, 2 * D), lambda b, i: (b, i, 0)),
        pl.BlockSpec((1, tm, PLE_DIM), lambda b, i: (b, i, 0)),
    ]
    in_specs += [_const_spec(a.shape) for a in weights]
    return pl.pallas_call(
        _post_body,
        grid=(B, S // tm),
        in_specs=in_specs,
        out_specs=pl.BlockSpec((1, tm, D), lambda b, i: (b, i, 0)),
        out_shape=jax.ShapeDtypeStruct((B, S, D), F32),
        compiler_params=pltpu.CompilerParams(
            dimension_semantics=("arbitrary", "arbitrary"), vmem_limit_bytes=VMEM_LIMIT_BYTES),
        name="post",
    )(x, oTna, oTmla, gates, p, *weights)


def _prep_weights(g_mix, w_in, g_q_lat, w_uq, g_kv_lat, w_ukv, w_na_o, w_mla_o, w_out,
                  g_ffn, w_gate_up, w_down, g_ple, w_ple_gate, w_ple, g_final):
    o = np.cumsum((0, NA_WIDTH, NA_WIDTH, NA_WIDTH, Q_LORA, KV_LORA, MLA_ROPE, 2 * D_MODEL))
    w_q, w_k, w_v, w_zq, w_zkv, w_zr, w_g = (w_in[:, o[i]:o[i + 1]] for i in range(7))
    half = MLA_ROPE // 2
    w_zr_rot = jnp.concatenate([-w_zr[:, half:], w_zr[:, :half]], axis=1)

    w_uq_h = w_uq.reshape(Q_LORA, MLA_HEADS, MLA_NOPE + MLA_ROPE)
    w_uq_pad = jnp.pad(w_uq_h, ((0, 0), (0, 0), (0, MLA_QK_PAD - MLA_NOPE - MLA_ROPE)))
    w_uqT = w_uq_pad.reshape(Q_LORA, MLA_HEADS * MLA_QK_PAD).T

    w_ukv_h = w_ukv.reshape(KV_LORA, MLA_HEADS, MLA_NOPE + MLA_V)
    w_uk = jnp.pad(w_ukv_h[:, :, :MLA_NOPE], ((0, 0), (0, 0), (0, MLA_QK_PAD - MLA_NOPE)))
    w_uk = w_uk.reshape(KV_LORA, MLA_HEADS * MLA_QK_PAD)
    w_uvT = w_ukv_h[:, :, MLA_NOPE:].reshape(KV_LORA, MLA_WIDTH).T

    place = np.zeros((MLA_ROPE, MLA_HEADS, MLA_QK_PAD), np.float32)
    for j in range(MLA_ROPE):
        place[j, :, MLA_NOPE + j] = 1.0
    place = place.reshape(MLA_ROPE, MLA_HEADS * MLA_QK_PAD)

    w_gate = w_gate_up[:, :D_FF]
    w_up = w_gate_up[:, D_FF:]
    w_dn = w_down

    row = lambda g: g.reshape(1, -1).astype(F32)
    return {
        "g_mix": row(g_mix),
        "w_kna": w_k.astype(BF16),
        "w_qnaT": (w_q * NA_HEAD_DIM ** -0.5).T.astype(BF16),
        "w_vnaT": w_v.T.astype(BF16),
        "w_zq": w_zq.astype(BF16),
        "w_zkv": w_zkv.astype(BF16),
        "w_zrr": jnp.concatenate([w_zr, w_zr_rot], axis=1).astype(BF16),
        "w_g": w_g.astype(BF16),
        "g_q": row(g_q_lat),
        "w_uqT": w_uqT.astype(BF16),
        "g_kv": row(g_kv_lat),
        "w_uk": w_uk.astype(BF16),
        "w_uvT": w_uvT.astype(BF16),
        "place": jnp.asarray(place, BF16),
        "w_na_o": w_na_o.astype(BF16),
        "w_mla_o": w_mla_o.astype(BF16),
        "w_out": w_out.astype(BF16),
        "g_ffn": row(g_ffn),
        "w_gate": w_gate.astype(BF16),
        "w_up": w_up.astype(BF16),
        "w_down": w_dn.astype(BF16),
        "g_ple": row(g_ple),
        "w_ple_gate": w_ple_gate.astype(BF16),
        "w_ple": w_ple.astype(BF16),
        "g_final": row(g_final),
    }


def _rope_tables(s_max):
    half = MLA_ROPE // 2
    pos = jnp.arange(s_max, dtype=F32)
    freqs = ROPE_THETA ** (-jnp.arange(half, dtype=F32) / half)
    ang = pos[:, None] * freqs[None, :]
    cos, sin = jnp.cos(ang), jnp.sin(ang)
    return {"cosT": cos.T, "sinT": sin.T,
            "cos": jnp.concatenate([cos, cos], axis=1), "sin": jnp.concatenate([sin, sin], axis=1)}


def _trunk(x, p, w, tabs, biasT):
    kna, qTna, vTna, gates, qT, k, vT = _inproj(x, w, tabs)
    oTna = _na(qTna, kna, vTna, biasT)
    oTmla = _mla(qT, k, vT)
    return _post(x, oTna, oTmla, gates, p, w)


def kernel(x_prompt, x_sample, p_prompt, p_sample, g_mix, w_in, rpb, g_q_lat, w_uq, g_kv_lat, w_ukv,
           w_na_o, w_mla_o, w_out, g_ffn, w_gate_up, w_down, g_ple, w_ple_gate, w_ple, g_final):
    assert g_mix.shape[0] == 1, "single-layer trunk"
    w = _prep_weights(g_mix[0], w_in[0], g_q_lat[0], w_uq[0], g_kv_lat[0], w_ukv[0], w_na_o[0], w_mla_o[0],
                      w_out[0], g_ffn[0], w_gate_up[0], w_down[0], g_ple[0], w_ple_gate[0], w_ple[0], g_final)
    biasT = _na_bias_table(rpb[0])
    tabs = _rope_tables(max(x_prompt.shape[1], x_sample.shape[1]))
    y_prompt = _trunk(x_prompt, p_prompt[0], w, tabs, biasT)
    y_sample = _trunk(x_sample, p_sample[0], w, tabs, biasT)
    return (y_prompt, y_sample)
```

```python
import functools

import numpy as np
import jax
import jax.numpy as jnp
from jax import lax
from jax.experimental import pallas as pl
from jax.experimental.pallas import tpu as pltpu

D_MODEL = 1024
GRID_W = 64
NA_HEADS = 8
NA_HEAD_DIM = 64
NA_WIN_H = 8
NA_WIN_W = 16
MLA_HEADS = 8
MLA_NOPE = 64
MLA_ROPE = 32
MLA_V = 64
Q_LORA = 384
KV_LORA = 256
ROPE_THETA = 10000.0
D_FF = 2816
PLE_DIM = 256
EPS = 1e-6
NA_WIDTH = NA_HEADS * NA_HEAD_DIM
MLA_WIDTH = MLA_HEADS * MLA_V

MLA_QK_PAD = 128
MLA_V_PAD = 80
NA_V_PAD = 80
LOG2_E = 1.4426950408889634
NA_GROUP_ROWS = 4
NA_KEY_ROWS = 12
NA_UNROLL = 32
NA_SCORE_BUFS = 4
NA_LOOKAHEAD = 2
NA_V_CHUNK = 256
MASK_VALUE = -1e30

TM_IN = 512
TM_POST = 512
MLA_TQ = 512
MLA_TK = 256
MLA_UNROLL = 64
MLA_SCORE_BUFS = 4
MLA_LOOKAHEAD = 2
VMEM_LIMIT_BYTES = 56 * 1024 * 1024

BF16 = jnp.bfloat16
F32 = jnp.float32


def _dot(a, b):
    return jnp.dot(a, b, preferred_element_type=F32)


def _dot_nt(a, b):
    return lax.dot_general(a, b, (((1,), (1,)), ((), ())), preferred_element_type=F32)


def _dot_tn(a, b):
    return lax.dot_general(a, b, (((0,), (0,)), ((), ())), preferred_element_type=F32)


def _rms(x, g):
    ms = jnp.mean(x * x, axis=-1, keepdims=True)
    return x * lax.rsqrt(ms + EPS) * g


def _const_spec(shape):
    nd = len(shape)
    return pl.BlockSpec(shape, lambda *_: (0,) * nd, pipeline_mode=pl.Buffered(1))


def _chunked_spec(rows, tm, chunk):
    if tm >= chunk:
        return pl.BlockSpec((1, tm // chunk, rows, chunk), lambda b, i: (b, i, 0, 0))
    sub = chunk // tm
    return pl.BlockSpec((1, 1, rows, tm), lambda b, i: (b, i // sub, 0, i % sub))


def _store_chunked(ref, r0, val):
    width = ref.shape[3]
    for c in range(ref.shape[1]):
        ref[0, c, r0:r0 + val.shape[0], :] = val if val.shape[1] == width else val[:, c * width:(c + 1) * width]


def _inproj_body(x_ref, gmix_ref, wkna_ref, wqnaT_ref, wvnaT_ref, wz_ref, wg_ref,
                 gq_ref, wuqT_ref, gkv_ref, wuk_ref, wuvT_ref, place_ref,
                 cosT_ref, sinT_ref, cos_ref, sin_ref,
                 kna_ref, qTna_ref, vTna_ref, gates_ref, qT_ref, k_ref, vT_ref):
    tm = x_ref.shape[1]
    h = _rms(x_ref[0], gmix_ref[...]).astype(BF16)

    z = _dot(h, wz_ref[...])
    zqn = _rms(z[:, :Q_LORA], gq_ref[...]).astype(BF16)
    zkvn = _rms(z[:, Q_LORA:Q_LORA + KV_LORA], gkv_ref[...]).astype(BF16)
    zrr = z[:, Q_LORA + KV_LORA:]

    gates_ref[0] = jax.nn.sigmoid(_dot(h, wg_ref[...]))

    assert vT_ref.shape[3] == vTna_ref.shape[3]
    ones_rows = (lax.broadcasted_iota(jnp.int32, (NA_V_PAD - NA_HEAD_DIM, vT_ref.shape[3]), 0) == 0
                 ).astype(F32).astype(BF16)

    scale = (MLA_NOPE + MLA_ROPE) ** -0.5 * LOG2_E
    qT = _dot_nt(wuqT_ref[...], zqn)
    cT = cosT_ref[...]
    sT = sinT_ref[...]
    half = MLA_ROPE // 2
    for hh in range(MLA_HEADS):
        base = hh * MLA_QK_PAD
        nope = qT[base:base + MLA_NOPE]
        x1 = qT[base + MLA_NOPE:base + MLA_NOPE + half]
        x2 = qT[base + MLA_NOPE + half:base + MLA_NOPE + MLA_ROPE]
        _store_chunked(qT_ref, base, (nope * scale).astype(BF16))
        _store_chunked(qT_ref, base + MLA_NOPE, ((x1 * cT - x2 * sT) * scale).astype(BF16))
        _store_chunked(qT_ref, base + MLA_NOPE + half, ((x2 * cT + x1 * sT) * scale).astype(BF16))
        _store_chunked(qT_ref, base + MLA_NOPE + MLA_ROPE, jnp.zeros((MLA_QK_PAD - MLA_NOPE - MLA_ROPE, tm), BF16))

    k_rope = zrr[:, :MLA_ROPE] * cos_ref[...] + zrr[:, MLA_ROPE:] * sin_ref[...]
    k_nope = _dot(zkvn, wuk_ref[...])
    k_ref[0] = (k_nope + _dot(k_rope.astype(BF16), place_ref[...])).astype(BF16)

    vT = _dot_nt(wuvT_ref[...], zkvn).astype(BF16)
    for hh in range(MLA_HEADS):
        _store_chunked(vT_ref, hh * MLA_V_PAD, vT[hh * MLA_V:(hh + 1) * MLA_V])
        _store_chunked(vT_ref, hh * MLA_V_PAD + MLA_V, ones_rows)

    kna_ref[0] = _dot(h, wkna_ref[...]).astype(BF16)
    _store_chunked(qTna_ref, 0, (_dot_nt(wqnaT_ref[...], h) * LOG2_E).astype(BF16))
    vTna = _dot_nt(wvnaT_ref[...], h).astype(BF16)
    for hh in range(NA_HEADS):
        _store_chunked(vTna_ref, hh * NA_V_PAD, vTna[hh * NA_HEAD_DIM:(hh + 1) * NA_HEAD_DIM])
        _store_chunked(vTna_ref, hh * NA_V_PAD + NA_HEAD_DIM, ones_rows)


def _inproj(x, w, tabs):
    B, S, D = x.shape
    tm = TM_IN
    nt = S // tm
    assert NA_V_PAD - NA_HEAD_DIM == MLA_V_PAD - MLA_V
    hq = MLA_HEADS * MLA_QK_PAD
    na_nq = NA_GROUP_ROWS * GRID_W
    weights = (w["g_mix"], w["w_kna"], w["w_qnaT"], w["w_vnaT"], w["w_z"], w["w_g"],
               w["g_q"], w["w_uqT"], w["g_kv"], w["w_uk"], w["w_uvT"], w["place"])
    in_specs = [pl.BlockSpec((1, tm, D), lambda b, i: (b, i, 0))]
    in_specs += [_const_spec(a.shape) for a in weights]
    in_specs += [
        pl.BlockSpec((MLA_ROPE // 2, tm), lambda b, i: (0, i)),
        pl.BlockSpec((MLA_ROPE // 2, tm), lambda b, i: (0, i)),
        pl.BlockSpec((tm, MLA_ROPE), lambda b, i: (i, 0)),
        pl.BlockSpec((tm, MLA_ROPE), lambda b, i: (i, 0)),
    ]
    out_shape = (
        jax.ShapeDtypeStruct((B, S, NA_WIDTH), BF16),
        jax.ShapeDtypeStruct((B, S // na_nq, NA_WIDTH, na_nq), BF16),
        jax.ShapeDtypeStruct((B, S // NA_V_CHUNK, NA_HEADS * NA_V_PAD, NA_V_CHUNK), BF16),
        jax.ShapeDtypeStruct((B, S, 2 * D), F32),
        jax.ShapeDtypeStruct((B, S // MLA_TQ, hq, MLA_TQ), BF16),
        jax.ShapeDtypeStruct((B, S, hq), BF16),
        jax.ShapeDtypeStruct((B, S // MLA_TK, MLA_HEADS * MLA_V_PAD, MLA_TK), BF16),
    )
    out_specs = (
        pl.BlockSpec((1, tm, NA_WIDTH), lambda b, i: (b, i, 0)),
        _chunked_spec(NA_WIDTH, tm, na_nq),
        _chunked_spec(NA_HEADS * NA_V_PAD, tm, NA_V_CHUNK),
        pl.BlockSpec((1, tm, 2 * D), lambda b, i: (b, i, 0)),
        _chunked_spec(hq, tm, MLA_TQ),
        pl.BlockSpec((1, tm, hq), lambda b, i: (b, i, 0)),
        _chunked_spec(MLA_HEADS * MLA_V_PAD, tm, MLA_TK),
    )
    return pl.pallas_call(
        _inproj_body,
        grid=(B, nt),
        in_specs=in_specs,
        out_specs=out_specs,
        out_shape=out_shape,
        compiler_params=pltpu.CompilerParams(
            dimension_semantics=("arbitrary", "arbitrary"), vmem_limit_bytes=VMEM_LIMIT_BYTES),
        name="inproj",
    )(x, *weights, tabs["cosT"][:, :S], tabs["sinT"][:, :S], tabs["cos"][:S], tabs["sin"][:S])


def _na_body(qT_ref, k_ref, vT_ref, bias_ref, o_ref, *s_refs, rows, unroll):
    ng, nq = qT_ref.shape[1], qT_ref.shape[3]
    nk = NA_KEY_ROWS * GRID_W
    nbuf = len(s_refs)
    row = lax.broadcasted_iota(jnp.int32, (2 * NA_HEAD_DIM, nq), 0)

    def key_row0(g):
        return jnp.clip(NA_GROUP_ROWS * g - NA_WIN_H // 2, 0, rows - NA_KEY_ROWS)

    def scores_into(g, hh, s_ref):
        tok0 = pl.multiple_of(key_row0(g) * GRID_W, NA_V_CHUNK)
        kwin = k_ref[0, pl.ds(tok0, nk), :]
        qT = qT_ref[0, g]
        lo = hh * NA_HEAD_DIM
        q_h = jnp.where((row >= lo) & (row < lo + NA_HEAD_DIM), qT, jnp.zeros_like(qT))
        kind = jnp.where(g == 0, 0, jnp.where(g == ng - 1, 2, 1))
        s = _dot(kwin, q_h) + bias_ref[kind, hh]
        s_ref[...] = s
        return jnp.max(s, axis=0, keepdims=True)

    def softmax_pv(g, hh, s_ref, mx):
        c0 = lax.shift_right_logical(key_row0(g), 2)
        pb = jnp.exp2(s_ref[...] - mx).astype(BF16)
        acc = jnp.zeros((NA_V_PAD, nq), F32)
        for c in range(nk // NA_V_CHUNK):
            v_c = vT_ref[0, c0 + c, hh * NA_V_PAD:(hh + 1) * NA_V_PAD, :]
            acc = acc + _dot(v_c, pb[c * NA_V_CHUNK:(c + 1) * NA_V_CHUNK])
        o_ref[0, g, hh * NA_HEAD_DIM:(hh + 1) * NA_HEAD_DIM, :] = (
            acc[:NA_HEAD_DIM] / acc[NA_HEAD_DIM:NA_HEAD_DIM + 1]).astype(BF16)

    def trip(i, mxs):
        g0 = i * (unroll // 2)
        for u in range(unroll):
            ahead = u + NA_LOOKAHEAD
            g_ahead = jnp.minimum(g0 + ahead // 2, ng - 1)
            mx_new = scores_into(g_ahead, ahead % 2, s_refs[ahead % nbuf])
            softmax_pv(g0 + u // 2, u % 2, s_refs[u % nbuf], mxs[0])
            mxs = mxs[1:] + (mx_new,)
        return mxs

    mxs = tuple(scores_into(t // 2, t % 2, s_refs[t]) for t in range(NA_LOOKAHEAD))
    lax.fori_loop(0, 2 * ng // unroll, trip, mxs)


def _na(qT, k, vT, biasT):
    B, ng, _, nq = qT.shape
    S = k.shape[1]
    rows = S // GRID_W
    nk = NA_KEY_ROWS * GRID_W
    pair = 2 * NA_HEAD_DIM
    assert nq == NA_GROUP_ROWS * GRID_W and rows >= NA_KEY_ROWS
    unroll = next(u for u in range(NA_UNROLL, 0, -NA_SCORE_BUFS) if (2 * ng) % u == 0)
    assert NA_UNROLL % NA_SCORE_BUFS == 0 and NA_SCORE_BUFS % 2 == 0 and NA_LOOKAHEAD < NA_SCORE_BUFS <= unroll
    return pl.pallas_call(
        functools.partial(_na_body, rows=rows, unroll=unroll),
        grid=(B, NA_HEADS // 2),
        in_specs=[
            pl.BlockSpec((1, ng, pair, nq), lambda b, hp: (b, 0, hp, 0)),
            pl.BlockSpec((1, S, pair), lambda b, hp: (b, 0, hp)),
            pl.BlockSpec((1, S // NA_V_CHUNK, 2 * NA_V_PAD, NA_V_CHUNK), lambda b, hp: (b, 0, hp, 0)),
            pl.BlockSpec((3, 2, nk, nq), lambda b, hp: (0, hp, 0, 0)),
        ],
        out_specs=pl.BlockSpec((1, ng, pair, nq), lambda b, hp: (b, 0, hp, 0)),
        out_shape=jax.ShapeDtypeStruct((B, ng, NA_WIDTH, nq), BF16),
        scratch_shapes=[pltpu.VMEM((nk, nq), F32)] * NA_SCORE_BUFS,
        compiler_params=pltpu.CompilerParams(
            dimension_semantics=("arbitrary", "arbitrary"), vmem_limit_bytes=VMEM_LIMIT_BYTES),
        name="natten",
    )(qT, k, vT, biasT)


def _na_bias_table(rpb):
    n_off_r, n_off_c = 2 * NA_WIN_H - 1, 2 * NA_WIN_W - 1
    cols = np.arange(GRID_W)
    cs = np.clip(cols - NA_WIN_W // 2, 0, GRID_W - NA_WIN_W)
    col_ok = (cols[:, None] >= cs[None, :]) & (cols[:, None] < cs[None, :] + NA_WIN_W)
    col_idx = np.clip(cols[:, None] - cols[None, :] + NA_WIN_W - 1, 0, n_off_c - 1)
    onehot = (col_idx[None] == np.arange(n_off_c)[:, None, None]) & col_ok[None]
    onehot = jnp.asarray(onehot.reshape(n_off_c, GRID_W * GRID_W), F32)
    blocks = jnp.dot(rpb.reshape(NA_HEADS * n_off_r, n_off_c), onehot, precision=lax.Precision.HIGHEST)
    blocks = blocks.reshape(NA_HEADS, n_off_r, GRID_W, GRID_W)
    blocks = jnp.where(jnp.asarray(col_ok)[None, None], blocks * LOG2_E, MASK_VALUE)
    masked = jnp.full((NA_HEADS, 1, GRID_W, GRID_W), MASK_VALUE, F32)
    blocks = jnp.concatenate([blocks, masked], axis=1)

    rows = 4 * NA_KEY_ROWS
    kinds = ((0, 0), (NA_KEY_ROWS, NA_KEY_ROWS - NA_WIN_H // 2), (rows - NA_GROUP_ROWS, rows - NA_KEY_ROWS))
    sel = np.zeros((3, NA_KEY_ROWS, NA_GROUP_ROWS), np.int32)
    for t, (r0, ws) in enumerate(kinds):
        q_row = r0 + np.arange(NA_GROUP_ROWS)
        k_row = ws + np.arange(NA_KEY_ROWS)
        rs = np.clip(q_row - NA_WIN_H // 2, 0, rows - NA_WIN_H)
        row_ok = (k_row[:, None] >= rs[None, :]) & (k_row[:, None] < rs[None, :] + NA_WIN_H)
        sel[t] = np.where(row_ok, k_row[:, None] - q_row[None, :] + NA_WIN_H - 1, n_off_r)
    table = jnp.take(blocks, jnp.asarray(sel.reshape(-1)), axis=1)
    table = table.reshape(NA_HEADS, 3, NA_KEY_ROWS, NA_GROUP_ROWS, GRID_W, GRID_W)
    table = table.transpose(1, 0, 2, 4, 3, 5)
    return table.reshape(3, NA_HEADS, NA_KEY_ROWS * GRID_W, NA_GROUP_ROWS * GRID_W)


def _mla_body(qT_ref, k_ref, vT_ref, o_ref, *s_refs, unroll):
    nq, tq = qT_ref.shape[1], qT_ref.shape[3]
    nchunk, tk = vT_ref.shape[1], vT_ref.shape[3]
    nbuf = len(s_refs)

    def scores_into(qb, j, s_ref):
        s = _dot(k_ref[0, j * tk:(j + 1) * tk, :], qT_ref[0, qb])
        s_ref[...] = s
        return jnp.max(s, axis=0, keepdims=True)

    def softmax_pv(j, s_ref, mx, m, acc):
        m_new = jnp.maximum(m, mx)
        alpha = jnp.exp2(m - m_new)
        p = jnp.exp2(s_ref[...] - m_new).astype(BF16)
        return m_new, alpha * acc + _dot(vT_ref[0, j], p)

    blocks_per_trip = unroll // nchunk

    def trip(i, mxs):
        qb0 = i * blocks_per_trip
        qb1 = jnp.minimum(qb0 + blocks_per_trip, nq - blocks_per_trip)
        for u in range(unroll):
            if u % nchunk == 0:
                m, acc = jnp.full((1, tq), -jnp.inf, F32), jnp.zeros((MLA_V_PAD, tq), F32)
            ahead = u + MLA_LOOKAHEAD
            qb_ahead = (qb0 if ahead < unroll else qb1) + (ahead % unroll) // nchunk
            mx_new = scores_into(qb_ahead, ahead % nchunk, s_refs[ahead % nbuf])
            m, acc = softmax_pv(u % nchunk, s_refs[u % nbuf], mxs[0], m, acc)
            mxs = mxs[1:] + (mx_new,)
            if (u + 1) % nchunk == 0:
                o_ref[0, qb0 + u // nchunk] = (acc[:MLA_V] / acc[MLA_V:MLA_V + 1]).astype(BF16)
        return mxs

    mxs = tuple(scores_into(0, t, s_refs[t]) for t in range(MLA_LOOKAHEAD))
    lax.fori_loop(0, nq // blocks_per_trip, trip, mxs)


def _mla(qT, k, vT):
    B, nq, _, tq = qT.shape
    S = k.shape[1]
    nchunk, tk = vT.shape[1], vT.shape[3]
    unroll = next(u for u in range(MLA_UNROLL, 0, -MLA_SCORE_BUFS)
                  if u % nchunk == 0 and nq % (u // nchunk) == 0)
    assert MLA_UNROLL % MLA_SCORE_BUFS == 0 and MLA_LOOKAHEAD < MLA_SCORE_BUFS <= unroll
    return pl.pallas_call(
        functools.partial(_mla_body, unroll=unroll),
        grid=(B, MLA_HEADS),
        in_specs=[
            pl.BlockSpec((1, nq, MLA_QK_PAD, tq), lambda b, h: (b, 0, h, 0)),
            pl.BlockSpec((1, S, MLA_QK_PAD), lambda b, h: (b, 0, h)),
            pl.BlockSpec((1, nchunk, MLA_V_PAD, tk), lambda b, h: (b, 0, h, 0)),
        ],
        out_specs=pl.BlockSpec((1, nq, MLA_V, tq), lambda b, h: (b, 0, h, 0)),
        out_shape=jax.ShapeDtypeStruct((B, nq, MLA_WIDTH, tq), BF16),
        scratch_shapes=[pltpu.VMEM((tk, tq), F32)] * MLA_SCORE_BUFS,
        compiler_params=pltpu.CompilerParams(
            dimension_semantics=("arbitrary", "arbitrary"), vmem_limit_bytes=VMEM_LIMIT_BYTES),
        name="mla",
    )(qT, k, vT)


def _post_body(x_ref, oTna_ref, oTmla_ref, gates_ref, p_ref,
               wnao_ref, wmlao_ref, wout_ref, gffn_ref, wgate_ref, wup_ref, wdown_ref,
               gple_ref, wpleg_ref, wple_ref, gfin_ref, y_ref):
    d = x_ref.shape[2]
    oTna = jnp.concatenate([oTna_ref[0, c] for c in range(oTna_ref.shape[1])], axis=1)
    a = _dot_tn(oTna, wnao_ref[...])
    oTmla = jnp.concatenate([oTmla_ref[0, c] for c in range(oTmla_ref.shape[1])], axis=1)
    b = _dot_tn(oTmla, wmlao_ref[...])
    gates = gates_ref[0]
    mix = (gates[:, :d] * a + gates[:, d:] * b).astype(BF16)
    x1 = x_ref[0] + _dot(mix, wout_ref[...])
    ple = _dot(p_ref[0].astype(BF16), wple_ref[...])

    hn = _rms(x1, gffn_ref[...]).astype(BF16)
    gate = _dot(hn, wgate_ref[...])
    up = _dot(hn, wup_ref[...])
    act = (gate * jax.nn.sigmoid(gate) * up).astype(BF16)
    x2 = x1 + _dot(act, wdown_ref[...])

    ple_gate = jax.nn.sigmoid(_dot(_rms(x2, gple_ref[...]).astype(BF16), wpleg_ref[...]))
    x3 = x2 + ple_gate * ple
    y_ref[0] = _rms(x3, gfin_ref[...])


def _post(x, oTna, oTmla, gates, p, w):
    B, S, D = x.shape
    tm = TM_POST
    weights = (w["w_na_o"], w["w_mla_o"], w["w_out"], w["g_ffn"], w["w_gate"], w["w_up"], w["w_down"],
               w["g_ple"], w["w_ple_gate"], w["w_ple"], w["g_final"])
    in_specs = [
        pl.BlockSpec((1, tm, D), lambda b, i: (b, i, 0)),
        _chunked_spec(NA_WIDTH, tm, NA_GROUP_ROWS * GRID_W),
        _chunked_spec(MLA_WIDTH, tm, MLA_TQ),
        pl.BlockSpec((1, tm, 2 * D), lambda b, i: (b, i, 0)),
        pl.BlockSpec((1, tm, PLE_DIM), lambda b, i: (b, i, 0)),
    ]
    in_specs += [_const_spec(a.shape) for a in weights]
    return pl.pallas_call(
        _post_body,
        grid=(B, S // tm),
        in_specs=in_specs,
        out_specs=pl.BlockSpec((1, tm, D), lambda b, i: (b, i, 0)),
        out_shape=jax.ShapeDtypeStruct((B, S, D), F32),
        compiler_params=pltpu.CompilerParams(
            dimension_semantics=("arbitrary", "arbitrary"), vmem_limit_bytes=VMEM_LIMIT_BYTES),
        name="post",
    )(x, oTna, oTmla, gates, p, *weights)


def _prep_weights(g_mix, w_in, g_q_lat, w_uq, g_kv_lat, w_ukv, w_na_o, w_mla_o, w_out,
                  g_ffn, w_gate_up, w_down, g_ple, w_ple_gate, w_ple, g_final):
    o = np.cumsum((0, NA_WIDTH, NA_WIDTH, NA_WIDTH, Q_LORA, KV_LORA, MLA_ROPE, 2 * D_MODEL))
    w_q, w_k, w_v, w_zq, w_zkv, w_zr, w_g = (w_in[:, o[i]:o[i + 1]] for i in range(7))
    half = MLA_ROPE // 2
    w_zr_rot = jnp.concatenate([-w_zr[:, half:], w_zr[:, :half]], axis=1)

    w_uq_h = w_uq.reshape(Q_LORA, MLA_HEADS, MLA_NOPE + MLA_ROPE)
    w_uq_pad = jnp.pad(w_uq_h, ((0, 0), (0, 0), (0, MLA_QK_PAD - MLA_NOPE - MLA_ROPE)))
    w_uqT = w_uq_pad.reshape(Q_LORA, MLA_HEADS * MLA_QK_PAD).T

    w_ukv_h = w_ukv.reshape(KV_LORA, MLA_HEADS, MLA_NOPE + MLA_V)
    w_uk = jnp.pad(w_ukv_h[:, :, :MLA_NOPE], ((0, 0), (0, 0), (0, MLA_QK_PAD - MLA_NOPE)))
    w_uk = w_uk.reshape(KV_LORA, MLA_HEADS * MLA_QK_PAD)
    w_uvT = w_ukv_h[:, :, MLA_NOPE:].reshape(KV_LORA, MLA_WIDTH).T

    place = np.zeros((MLA_ROPE, MLA_HEADS, MLA_QK_PAD), np.float32)
    for j in range(MLA_ROPE):
        place[j, :, MLA_NOPE + j] = 1.0
    place = place.reshape(MLA_ROPE, MLA_HEADS * MLA_QK_PAD)

    w_gate = w_gate_up[:, :D_FF]
    w_up = w_gate_up[:, D_FF:]
    w_dn = w_down

    row = lambda g: g.reshape(1, -1).astype(F32)
    return {
        "g_mix": row(g_mix),
        "w_kna": w_k.astype(BF16),
        "w_qnaT": (w_q * NA_HEAD_DIM ** -0.5).T.astype(BF16),
        "w_vnaT": w_v.T.astype(BF16),
        "w_z": jnp.concatenate([w_zq, w_zkv, w_zr, w_zr_rot], axis=1).astype(BF16),
        "w_g": w_g.astype(BF16),
        "g_q": row(g_q_lat),
        "w_uqT": w_uqT.astype(BF16),
        "g_kv": row(g_kv_lat),
        "w_uk": w_uk.astype(BF16),
        "w_uvT": w_uvT.astype(BF16),
        "place": jnp.asarray(place, BF16),
        "w_na_o": w_na_o.astype(BF16),
        "w_mla_o": w_mla_o.astype(BF16),
        "w_out": w_out.astype(BF16),
        "g_ffn": row(g_ffn),
        "w_gate": w_gate.astype(BF16),
        "w_up": w_up.astype(BF16),
        "w_down": w_dn.astype(BF16),
        "g_ple": row(g_ple),
        "w_ple_gate": w_ple_gate.astype(BF16),
        "w_ple": w_ple.astype(BF16),
        "g_final": row(g_final),
    }


def _rope_tables(s_max):
    half = MLA_ROPE // 2
    pos = jnp.arange(s_max, dtype=F32)
    freqs = ROPE_THETA ** (-jnp.arange(half, dtype=F32) / half)
    ang = pos[:, None] * freqs[None, :]
    cos, sin = jnp.cos(ang), jnp.sin(ang)
    return {"cosT": cos.T, "sinT": sin.T,
            "cos": jnp.concatenate([cos, cos], axis=1), "sin": jnp.concatenate([sin, sin], axis=1)}


def _trunk(x, p, w, tabs, biasT):
    kna, qTna, vTna, gates, qT, k, vT = _inproj(x, w, tabs)
    oTna = _na(qTna, kna, vTna, biasT)
    oTmla = _mla(qT, k, vT)
    return _post(x, oTna, oTmla, gates, p, w)


def kernel(x_prompt, x_sample, p_prompt, p_sample, g_mix, w_in, rpb, g_q_lat, w_uq, g_kv_lat, w_ukv,
           w_na_o, w_mla_o, w_out, g_ffn, w_gate_up, w_down, g_ple, w_ple_gate, w_ple, g_final):
    assert g_mix.shape[0] == 1, "single-layer trunk"
    w = _prep_weights(g_mix[0], w_in[0], g_q_lat[0], w_uq[0], g_kv_lat[0], w_ukv[0], w_na_o[0], w_mla_o[0],
                      w_out[0], g_ffn[0], w_gate_up[0], w_down[0], g_ple[0], w_ple_gate[0], w_ple[0], g_final)
    biasT = _na_bias_table(rpb[0])
    tabs = _rope_tables(max(x_prompt.shape[1], x_sample.shape[1]))
    y_prompt = _trunk(x_prompt, p_prompt[0], w, tabs, biasT)
    y_sample = _trunk(x_sample, p_sample[0], w, tabs, biasT)
    return (y_prompt, y_sample)
```

```python
import functools

import numpy as np
import jax
import jax.numpy as jnp
from jax import lax
from jax.experimental import pallas as pl
from jax.experimental.pallas import tpu as pltpu

D_MODEL = 1024
GRID_W = 64
NA_HEADS = 8
NA_HEAD_DIM = 64
NA_WIN_H = 8
NA_WIN_W = 16
MLA_HEADS = 8
MLA_NOPE = 64
MLA_ROPE = 32
MLA_V = 64
Q_LORA = 384
KV_LORA = 256
ROPE_THETA = 10000.0
D_FF = 2816
PLE_DIM = 256
EPS = 1e-6
NA_WIDTH = NA_HEADS * NA_HEAD_DIM
MLA_WIDTH = MLA_HEADS * MLA_V

MLA_QK_PAD = 128
MLA_V_PAD = 80
NA_V_PAD = 80
LOG2_E = 1.4426950408889634
NA_GROUP_ROWS = 4
NA_KEY_ROWS = 12
NA_UNROLL = 32
NA_SCORE_BUFS = 4
NA_LOOKAHEAD = 3
NA_V_CHUNK = 256
MASK_VALUE = -1e30

TM_IN = 512
TM_POST = 512
MLA_TQ = 512
MLA_TK = 256
MLA_UNROLL = 64
MLA_SCORE_BUFS = 4
MLA_LOOKAHEAD = 2
VMEM_LIMIT_BYTES = 56 * 1024 * 1024

BF16 = jnp.bfloat16
F32 = jnp.float32


def _dot(a, b):
    return jnp.dot(a, b, preferred_element_type=F32)


def _dot_nt(a, b):
    return lax.dot_general(a, b, (((1,), (1,)), ((), ())), preferred_element_type=F32)


def _dot_tn(a, b):
    return lax.dot_general(a, b, (((0,), (0,)), ((), ())), preferred_element_type=F32)


def _rms(x, g):
    ms = jnp.mean(x * x, axis=-1, keepdims=True)
    return x * lax.rsqrt(ms + EPS) * g


def _const_spec(shape):
    nd = len(shape)
    return pl.BlockSpec(shape, lambda *_: (0,) * nd, pipeline_mode=pl.Buffered(1))


def _chunked_spec(rows, tm, chunk):
    if tm >= chunk:
        return pl.BlockSpec((1, tm // chunk, rows, chunk), lambda b, i: (b, i, 0, 0))
    sub = chunk // tm
    return pl.BlockSpec((1, 1, rows, tm), lambda b, i: (b, i // sub, 0, i % sub))


def _store_chunked(ref, r0, val):
    width = ref.shape[3]
    for c in range(ref.shape[1]):
        ref[0, c, r0:r0 + val.shape[0], :] = val if val.shape[1] == width else val[:, c * width:(c + 1) * width]


def _inproj_body(x_ref, gmix_ref, wkna_ref, wqnaT_ref, wvnaT_ref, wz_ref, wg_ref,
                 gq_ref, wuqT_ref, gkv_ref, wuk_ref, wuvT_ref, place_ref,
                 cosT_ref, sinT_ref, cos_ref, sin_ref,
                 kna_ref, qTna_ref, vTna_ref, gates_ref, qT_ref, k_ref, vT_ref):
    tm = x_ref.shape[1]
    h = _rms(x_ref[0], gmix_ref[...]).astype(BF16)

    z = _dot(h, wz_ref[...])
    zqn = _rms(z[:, :Q_LORA], gq_ref[...]).astype(BF16)
    zkvn = _rms(z[:, Q_LORA:Q_LORA + KV_LORA], gkv_ref[...]).astype(BF16)
    zrr = z[:, Q_LORA + KV_LORA:]

    gates_ref[0] = jax.nn.sigmoid(_dot(h, wg_ref[...]))

    assert vT_ref.shape[3] == vTna_ref.shape[3]
    ones_rows = (lax.broadcasted_iota(jnp.int32, (NA_V_PAD - NA_HEAD_DIM, vT_ref.shape[3]), 0) == 0
                 ).astype(F32).astype(BF16)

    scale = (MLA_NOPE + MLA_ROPE) ** -0.5 * LOG2_E
    qT = _dot_nt(wuqT_ref[...], zqn)
    cT = cosT_ref[...]
    sT = sinT_ref[...]
    half = MLA_ROPE // 2
    for hh in range(MLA_HEADS):
        base = hh * MLA_QK_PAD
        nope = qT[base:base + MLA_NOPE]
        x1 = qT[base + MLA_NOPE:base + MLA_NOPE + half]
        x2 = qT[base + MLA_NOPE + half:base + MLA_NOPE + MLA_ROPE]
        _store_chunked(qT_ref, base, (nope * scale).astype(BF16))
        _store_chunked(qT_ref, base + MLA_NOPE, ((x1 * cT - x2 * sT) * scale).astype(BF16))
        _store_chunked(qT_ref, base + MLA_NOPE + half, ((x2 * cT + x1 * sT) * scale).astype(BF16))
        _store_chunked(qT_ref, base + MLA_NOPE + MLA_ROPE, jnp.zeros((MLA_QK_PAD - MLA_NOPE - MLA_ROPE, tm), BF16))

    k_rope = zrr[:, :MLA_ROPE] * cos_ref[...] + zrr[:, MLA_ROPE:] * sin_ref[...]
    k_nope = _dot(zkvn, wuk_ref[...])
    k_ref[0] = (k_nope + _dot(k_rope.astype(BF16), place_ref[...])).astype(BF16)

    vT = _dot_nt(wuvT_ref[...], zkvn).astype(BF16)
    for hh in range(MLA_HEADS):
        _store_chunked(vT_ref, hh * MLA_V_PAD, vT[hh * MLA_V:(hh + 1) * MLA_V])
        _store_chunked(vT_ref, hh * MLA_V_PAD + MLA_V, ones_rows)

    kna_ref[0] = _dot(h, wkna_ref[...]).astype(BF16)
    _store_chunked(qTna_ref, 0, (_dot_nt(wqnaT_ref[...], h) * LOG2_E).astype(BF16))
    vTna = _dot_nt(wvnaT_ref[...], h).astype(BF16)
    for hh in range(NA_HEADS):
        _store_chunked(vTna_ref, hh * NA_V_PAD, vTna[hh * NA_HEAD_DIM:(hh + 1) * NA_HEAD_DIM])
        _store_chunked(vTna_ref, hh * NA_V_PAD + NA_HEAD_DIM, ones_rows)


def _inproj(x, w, tabs):
    B, S, D = x.shape
    tm = TM_IN
    nt = S // tm
    assert NA_V_PAD - NA_HEAD_DIM == MLA_V_PAD - MLA_V
    hq = MLA_HEADS * MLA_QK_PAD
    na_nq = NA_GROUP_ROWS * GRID_W
    weights = (w["g_mix"], w["w_kna"], w["w_qnaT"], w["w_vnaT"], w["w_z"], w["w_g"],
               w["g_q"], w["w_uqT"], w["g_kv"], w["w_uk"], w["w_uvT"], w["place"])
    in_specs = [pl.BlockSpec((1, tm, D), lambda b, i: (b, i, 0))]
    in_specs += [_const_spec(a.shape) for a in weights]
    in_specs += [
        pl.BlockSpec((MLA_ROPE // 2, tm), lambda b, i: (0, i)),
        pl.BlockSpec((MLA_ROPE // 2, tm), lambda b, i: (0, i)),
        pl.BlockSpec((tm, MLA_ROPE), lambda b, i: (i, 0)),
        pl.BlockSpec((tm, MLA_ROPE), lambda b, i: (i, 0)),
    ]
    out_shape = (
        jax.ShapeDtypeStruct((B, S, NA_WIDTH), BF16),
        jax.ShapeDtypeStruct((B, S // na_nq, NA_WIDTH, na_nq), BF16),
        jax.ShapeDtypeStruct((B, S // NA_V_CHUNK, NA_HEADS * NA_V_PAD, NA_V_CHUNK), BF16),
        jax.ShapeDtypeStruct((B, S, 2 * D), F32),
        jax.ShapeDtypeStruct((B, S // MLA_TQ, hq, MLA_TQ), BF16),
        jax.ShapeDtypeStruct((B, S, hq), BF16),
        jax.ShapeDtypeStruct((B, S // MLA_TK, MLA_HEADS * MLA_V_PAD, MLA_TK), BF16),
    )
    out_specs = (
        pl.BlockSpec((1, tm, NA_WIDTH), lambda b, i: (b, i, 0)),
        _chunked_spec(NA_WIDTH, tm, na_nq),
        _chunked_spec(NA_HEADS * NA_V_PAD, tm, NA_V_CHUNK),
        pl.BlockSpec((1, tm, 2 * D), lambda b, i: (b, i, 0)),
        _chunked_spec(hq, tm, MLA_TQ),
        pl.BlockSpec((1, tm, hq), lambda b, i: (b, i, 0)),
        _chunked_spec(MLA_HEADS * MLA_V_PAD, tm, MLA_TK),
    )
    return pl.pallas_call(
        _inproj_body,
        grid=(B, nt),
        in_specs=in_specs,
        out_specs=out_specs,
        out_shape=out_shape,
        compiler_params=pltpu.CompilerParams(
            dimension_semantics=("arbitrary", "arbitrary"), vmem_limit_bytes=VMEM_LIMIT_BYTES),
        name="inproj",
    )(x, *weights, tabs["cosT"][:, :S], tabs["sinT"][:, :S], tabs["cos"][:S], tabs["sin"][:S])


def _na_body(qT_ref, k_ref, vT_ref, bias_ref, o_ref, *s_refs, rows, unroll):
    ng, nq = qT_ref.shape[1], qT_ref.shape[3]
    nk = NA_KEY_ROWS * GRID_W
    nbuf = len(s_refs)
    row = lax.broadcasted_iota(jnp.int32, (2 * NA_HEAD_DIM, nq), 0)

    def key_row0(g):
        return jnp.clip(NA_GROUP_ROWS * g - NA_WIN_H // 2, 0, rows - NA_KEY_ROWS)

    def scores_into(g, hh, s_ref):
        tok0 = pl.multiple_of(key_row0(g) * GRID_W, NA_V_CHUNK)
        kwin = k_ref[0, pl.ds(tok0, nk), :]
        qT = qT_ref[0, g]
        lo = hh * NA_HEAD_DIM
        q_h = jnp.where((row >= lo) & (row < lo + NA_HEAD_DIM), qT, jnp.zeros_like(qT))
        kind = jnp.where(g == 0, 0, jnp.where(g == ng - 1, 2, 1))
        s = _dot(kwin, q_h) + bias_ref[kind, hh]
        s_ref[...] = s
        return jnp.max(s, axis=0, keepdims=True)

    def softmax_pv(g, hh, s_ref, mx):
        c0 = lax.shift_right_logical(key_row0(g), 2)
        pb = jnp.exp2(s_ref[...] - mx).astype(BF16)
        acc = jnp.zeros((NA_V_PAD, nq), F32)
        for c in range(nk // NA_V_CHUNK):
            v_c = vT_ref[0, c0 + c, hh * NA_V_PAD:(hh + 1) * NA_V_PAD, :]
            acc = acc + _dot(v_c, pb[c * NA_V_CHUNK:(c + 1) * NA_V_CHUNK])
        o_ref[0, g, hh * NA_HEAD_DIM:(hh + 1) * NA_HEAD_DIM, :] = (
            acc[:NA_HEAD_DIM] / acc[NA_HEAD_DIM:NA_HEAD_DIM + 1]).astype(BF16)

    def trip(i, mxs):
        g0 = i * (unroll // 2)
        for u in range(unroll):
            ahead = u + NA_LOOKAHEAD
            g_ahead = jnp.minimum(g0 + ahead // 2, ng - 1)
            mx_new = scores_into(g_ahead, ahead % 2, s_refs[ahead % nbuf])
            softmax_pv(g0 + u // 2, u % 2, s_refs[u % nbuf], mxs[0])
            mxs = mxs[1:] + (mx_new,)
        return mxs

    mxs = tuple(scores_into(t // 2, t % 2, s_refs[t]) for t in range(NA_LOOKAHEAD))
    lax.fori_loop(0, 2 * ng // unroll, trip, mxs)


def _na(qT, k, vT, biasT):
    B, ng, _, nq = qT.shape
    S = k.shape[1]
    rows = S // GRID_W
    nk = NA_KEY_ROWS * GRID_W
    pair = 2 * NA_HEAD_DIM
    assert nq == NA_GROUP_ROWS * GRID_W and rows >= NA_KEY_ROWS
    unroll = next(u for u in range(NA_UNROLL, 0, -NA_SCORE_BUFS) if (2 * ng) % u == 0)
    assert NA_UNROLL % NA_SCORE_BUFS == 0 and NA_SCORE_BUFS % 2 == 0 and NA_LOOKAHEAD < NA_SCORE_BUFS <= unroll
    return pl.pallas_call(
        functools.partial(_na_body, rows=rows, unroll=unroll),
        grid=(B, NA_HEADS // 2),
        in_specs=[
            pl.BlockSpec((1, ng, pair, nq), lambda b, hp: (b, 0, hp, 0)),
            pl.BlockSpec((1, S, pair), lambda b, hp: (b, 0, hp)),
            pl.BlockSpec((1, S // NA_V_CHUNK, 2 * NA_V_PAD, NA_V_CHUNK), lambda b, hp: (b, 0, hp, 0)),
            pl.BlockSpec((3, 2, nk, nq), lambda b, hp: (0, hp, 0, 0)),
        ],
        out_specs=pl.BlockSpec((1, ng, pair, nq), lambda b, hp: (b, 0, hp, 0)),
        out_shape=jax.ShapeDtypeStruct((B, ng, NA_WIDTH, nq), BF16),
        scratch_shapes=[pltpu.VMEM((nk, nq), F32)] * NA_SCORE_BUFS,
        compiler_params=pltpu.CompilerParams(
            dimension_semantics=("arbitrary", "arbitrary"), vmem_limit_bytes=VMEM_LIMIT_BYTES),
        name="natten",
    )(qT, k, vT, biasT)


def _na_bias_table(rpb):
    n_off_r, n_off_c = 2 * NA_WIN_H - 1, 2 * NA_WIN_W - 1
    cols = np.arange(GRID_W)
    cs = np.clip(cols - NA_WIN_W // 2, 0, GRID_W - NA_WIN_W)
    col_ok = (cols[:, None] >= cs[None, :]) & (cols[:, None] < cs[None, :] + NA_WIN_W)
    col_idx = np.clip(cols[:, None] - cols[None, :] + NA_WIN_W - 1, 0, n_off_c - 1)
    onehot = (col_idx[None] == np.arange(n_off_c)[:, None, None]) & col_ok[None]
    onehot = jnp.asarray(onehot.reshape(n_off_c, GRID_W * GRID_W), F32)
    blocks = jnp.dot(rpb.reshape(NA_HEADS * n_off_r, n_off_c), onehot, precision=lax.Precision.HIGHEST)
    blocks = blocks.reshape(NA_HEADS, n_off_r, GRID_W, GRID_W)
    blocks = jnp.where(jnp.asarray(col_ok)[None, None], blocks * LOG2_E, MASK_VALUE)
    masked = jnp.full((NA_HEADS, 1, GRID_W, GRID_W), MASK_VALUE, F32)
    blocks = jnp.concatenate([blocks, masked], axis=1)

    rows = 4 * NA_KEY_ROWS
    kinds = ((0, 0), (NA_KEY_ROWS, NA_KEY_ROWS - NA_WIN_H // 2), (rows - NA_GROUP_ROWS, rows - NA_KEY_ROWS))
    sel = np.zeros((3, NA_KEY_ROWS, NA_GROUP_ROWS), np.int32)
    for t, (r0, ws) in enumerate(kinds):
        q_row = r0 + np.arange(NA_GROUP_ROWS)
        k_row = ws + np.arange(NA_KEY_ROWS)
        rs = np.clip(q_row - NA_WIN_H // 2, 0, rows - NA_WIN_H)
        row_ok = (k_row[:, None] >= rs[None, :]) & (k_row[:, None] < rs[None, :] + NA_WIN_H)
        sel[t] = np.where(row_ok, k_row[:, None] - q_row[None, :] + NA_WIN_H - 1, n_off_r)
    table = jnp.take(blocks, jnp.asarray(sel.reshape(-1)), axis=1)
    table = table.reshape(NA_HEADS, 3, NA_KEY_ROWS, NA_GROUP_ROWS, GRID_W, GRID_W)
    table = table.transpose(1, 0, 2, 4, 3, 5)
    return table.reshape(3, NA_HEADS, NA_KEY_ROWS * GRID_W, NA_GROUP_ROWS * GRID_W)


def _mla_body(qT_ref, k_ref, vT_ref, o_ref, *s_refs, unroll):
    nq, tq = qT_ref.shape[1], qT_ref.shape[3]
    nchunk, tk = vT_ref.shape[1], vT_ref.shape[3]
    nbuf = len(s_refs)

    def scores_into(qb, j, s_ref):
        s = _dot(k_ref[0, j * tk:(j + 1) * tk, :], qT_ref[0, qb])
        s_ref[...] = s
        return jnp.max(s, axis=0, keepdims=True)

    def softmax_pv(j, s_ref, mx, m, acc):
        m_new = jnp.maximum(m, mx)
        alpha = jnp.exp2(m - m_new)
        p = jnp.exp2(s_ref[...] - m_new).astype(BF16)
        return m_new, alpha * acc + _dot(vT_ref[0, j], p)

    blocks_per_trip = unroll // nchunk

    def trip(i, mxs):
        qb0 = i * blocks_per_trip
        qb1 = jnp.minimum(qb0 + blocks_per_trip, nq - blocks_per_trip)
        for u in range(unroll):
            if u % nchunk == 0:
                m, acc = jnp.full((1, tq), -jnp.inf, F32), jnp.zeros((MLA_V_PAD, tq), F32)
            ahead = u + MLA_LOOKAHEAD
            qb_ahead = (qb0 if ahead < unroll else qb1) + (ahead % unroll) // nchunk
            mx_new = scores_into(qb_ahead, ahead % nchunk, s_refs[ahead % nbuf])
            m, acc = softmax_pv(u % nchunk, s_refs[u % nbuf], mxs[0], m, acc)
            mxs = mxs[1:] + (mx_new,)
            if (u + 1) % nchunk == 0:
                o_ref[0, qb0 + u // nchunk] = (acc[:MLA_V] / acc[MLA_V:MLA_V + 1]).astype(BF16)
        return mxs

    mxs = tuple(scores_into(0, t, s_refs[t]) for t in range(MLA_LOOKAHEAD))
    lax.fori_loop(0, nq // blocks_per_trip, trip, mxs)


def _mla(qT, k, vT):
    B, nq, _, tq = qT.shape
    S = k.shape[1]
    nchunk, tk = vT.shape[1], vT.shape[3]
    unroll = next(u for u in range(MLA_UNROLL, 0, -MLA_SCORE_BUFS)
                  if u % nchunk == 0 and nq % (u // nchunk) == 0)
    assert MLA_UNROLL % MLA_SCORE_BUFS == 0 and MLA_LOOKAHEAD < MLA_SCORE_BUFS <= unroll
    return pl.pallas_call(
        functools.partial(_mla_body, unroll=unroll),
        grid=(B, MLA_HEADS),
        in_specs=[
            pl.BlockSpec((1, nq, MLA_QK_PAD, tq), lambda b, h: (b, 0, h, 0)),
            pl.BlockSpec((1, S, MLA_QK_PAD), lambda b, h: (b, 0, h)),
            pl.BlockSpec((1, nchunk, MLA_V_PAD, tk), lambda b, h: (b, 0, h, 0)),
        ],
        out_specs=pl.BlockSpec((1, nq, MLA_V, tq), lambda b, h: (b, 0, h, 0)),
        out_shape=jax.ShapeDtypeStruct((B, nq, MLA_WIDTH, tq), BF16),
        scratch_shapes=[pltpu.VMEM((tk, tq), F32)] * MLA_SCORE_BUFS,
        compiler_params=pltpu.CompilerParams(
            dimension_semantics=("arbitrary", "arbitrary"), vmem_limit_bytes=VMEM_LIMIT_BYTES),
        name="mla",
    )(qT, k, vT)


def _post_body(x_ref, oTna_ref, oTmla_ref, gates_ref, p_ref,
               wnao_ref, wmlao_ref, wout_ref, gffn_ref, wgate_ref, wup_ref, wdown_ref,
               gple_ref, wpleg_ref, wple_ref, gfin_ref, y_ref):
    d = x_ref.shape[2]
    oTna = jnp.concatenate([oTna_ref[0, c] for c in range(oTna_ref.shape[1])], axis=1)
    a = _dot_tn(oTna, wnao_ref[...])
    oTmla = jnp.concatenate([oTmla_ref[0, c] for c in range(oTmla_ref.shape[1])], axis=1)
    b = _dot_tn(oTmla, wmlao_ref[...])
    gates = gates_ref[0]
    mix = (gates[:, :d] * a + gates[:, d:] * b).astype(BF16)
    x1 = x_ref[0] + _dot(mix, wout_ref[...])
    ple = _dot(p_ref[0].astype(BF16), wple_ref[...])

    hn = _rms(x1, gffn_ref[...]).astype(BF16)
    gate = _dot(hn, wgate_ref[...])
    up = _dot(hn, wup_ref[...])
    act = (gate * jax.nn.sigmoid(gate) * up).astype(BF16)
    x2 = x1 + _dot(act, wdown_ref[...])

    ple_gate = jax.nn.sigmoid(_dot(_rms(x2, gple_ref[...]).astype(BF16), wpleg_ref[...]))
    x3 = x2 + ple_gate * ple
    y_ref[0] = _rms(x3, gfin_ref[...])


def _post(x, oTna, oTmla, gates, p, w):
    B, S, D = x.shape
    tm = TM_POST
    weights = (w["w_na_o"], w["w_mla_o"], w["w_out"], w["g_ffn"], w["w_gate"], w["w_up"], w["w_down"],
               w["g_ple"], w["w_ple_gate"], w["w_ple"], w["g_final"])
    in_specs = [
        pl.BlockSpec((1, tm, D), lambda b, i: (b, i, 0)),
        _chunked_spec(NA_WIDTH, tm, NA_GROUP_ROWS * GRID_W),
        _chunked_spec(MLA_WIDTH, tm, MLA_TQ),
        pl.BlockSpec((1, tm, 2 * D), lambda b, i: (b, i, 0)),
        pl.BlockSpec((1, tm, PLE_DIM), lambda b, i: (b, i, 0)),
    ]
    in_specs += [_const_spec(a.shape) for a in weights]
    return pl.pallas_call(
        _post_body,
        grid=(B, S // tm),
        in_specs=in_specs,
        out_specs=pl.BlockSpec((1, tm, D), lambda b, i: (b, i, 0)),
        out_shape=jax.ShapeDtypeStruct((B, S, D), F32),
        compiler_params=pltpu.CompilerParams(
            dimension_semantics=("arbitrary", "arbitrary"), vmem_limit_bytes=VMEM_LIMIT_BYTES),
        name="post",
    )(x, oTna, oTmla, gates, p, *weights)


def _prep_weights(g_mix, w_in, g_q_lat, w_uq, g_kv_lat, w_ukv, w_na_o, w_mla_o, w_out,
                  g_ffn, w_gate_up, w_down, g_ple, w_ple_gate, w_ple, g_final):
    o = np.cumsum((0, NA_WIDTH, NA_WIDTH, NA_WIDTH, Q_LORA, KV_LORA, MLA_ROPE, 2 * D_MODEL))
    w_q, w_k, w_v, w_zq, w_zkv, w_zr, w_g = (w_in[:, o[i]:o[i + 1]] for i in range(7))
    half = MLA_ROPE // 2
    w_zr_rot = jnp.concatenate([-w_zr[:, half:], w_zr[:, :half]], axis=1)

    w_uq_h = w_uq.reshape(Q_LORA, MLA_HEADS, MLA_NOPE + MLA_ROPE)
    w_uq_pad = jnp.pad(w_uq_h, ((0, 0), (0, 0), (0, MLA_QK_PAD - MLA_NOPE - MLA_ROPE)))
    w_uqT = w_uq_pad.reshape(Q_LORA, MLA_HEADS * MLA_QK_PAD).T

    w_ukv_h = w_ukv.reshape(KV_LORA, MLA_HEADS, MLA_NOPE + MLA_V)
    w_uk = jnp.pad(w_ukv_h[:, :, :MLA_NOPE], ((0, 0), (0, 0), (0, MLA_QK_PAD - MLA_NOPE)))
    w_uk = w_uk.reshape(KV_LORA, MLA_HEADS * MLA_QK_PAD)
    w_uvT = w_ukv_h[:, :, MLA_NOPE:].reshape(KV_LORA, MLA_WIDTH).T

    place = np.zeros((MLA_ROPE, MLA_HEADS, MLA_QK_PAD), np.float32)
    for j in range(MLA_ROPE):
        place[j, :, MLA_NOPE + j] = 1.0
    place = place.reshape(MLA_ROPE, MLA_HEADS * MLA_QK_PAD)

    w_gate = w_gate_up[:, :D_FF]
    w_up = w_gate_up[:, D_FF:]
    w_dn = w_down

    row = lambda g: g.reshape(1, -1).astype(F32)
    return {
        "g_mix": row(g_mix),
        "w_kna": w_k.astype(BF16),
        "w_qnaT": (w_q * NA_HEAD_DIM ** -0.5).T.astype(BF16),
        "w_vnaT": w_v.T.astype(BF16),
        "w_z": jnp.concatenate([w_zq, w_zkv, w_zr, w_zr_rot], axis=1).astype(BF16),
        "w_g": w_g.astype(BF16),
        "g_q": row(g_q_lat),
        "w_uqT": w_uqT.astype(BF16),
        "g_kv": row(g_kv_lat),
        "w_uk": w_uk.astype(BF16),
        "w_uvT": w_uvT.astype(BF16),
        "place": jnp.asarray(place, BF16),
        "w_na_o": w_na_o.astype(BF16),
        "w_mla_o": w_mla_o.astype(BF16),
        "w_out": w_out.astype(BF16),
        "g_ffn": row(g_ffn),
        "w_gate": w_gate.astype(BF16),
        "w_up": w_up.astype(BF16),
        "w_down": w_dn.astype(BF16),
        "g_ple": row(g_ple),
        "w_ple_gate": w_ple_gate.astype(BF16),
        "w_ple": w_ple.astype(BF16),
        "g_final": row(g_final),
    }


def _rope_tables(s_max):
    half = MLA_ROPE // 2
    pos = jnp.arange(s_max, dtype=F32)
    freqs = ROPE_THETA ** (-jnp.arange(half, dtype=F32) / half)
    ang = pos[:, None] * freqs[None, :]
    cos, sin = jnp.cos(ang), jnp.sin(ang)
    return {"cosT": cos.T, "sinT": sin.T,
            "cos": jnp.concatenate([cos, cos], axis=1), "sin": jnp.concatenate([sin, sin], axis=1)}


def _trunk(x, p, w, tabs, biasT):
    kna, qTna, vTna, gates, qT, k, vT = _inproj(x, w, tabs)
    oTna = _na(qTna, kna, vTna, biasT)
    oTmla = _mla(qT, k, vT)
    return _post(x, oTna, oTmla, gates, p, w)


def kernel(x_prompt, x_sample, p_prompt, p_sample, g_mix, w_in, rpb, g_q_lat, w_uq, g_kv_lat, w_ukv,
           w_na_o, w_mla_o, w_out, g_ffn, w_gate_up, w_down, g_ple, w_ple_gate, w_ple, g_final):
    assert g_mix.shape[0] == 1, "single-layer trunk"
    w = _prep_weights(g_mix[0], w_in[0], g_q_lat[0], w_uq[0], g_kv_lat[0], w_ukv[0], w_na_o[0], w_mla_o[0],
                      w_out[0], g_ffn[0], w_gate_up[0], w_down[0], g_ple[0], w_ple_gate[0], w_ple[0], g_final)
    biasT = _na_bias_table(rpb[0])
    tabs = _rope_tables(max(x_prompt.shape[1], x_sample.shape[1]))
    y_prompt = _trunk(x_prompt, p_prompt[0], w, tabs, biasT)
    y_sample = _trunk(x_sample, p_sample[0], w, tabs, biasT)
    return (y_prompt, y_sample)
```

```python
import functools

import numpy as np
import jax
import jax.numpy as jnp
from jax import lax
from jax.experimental import pallas as pl
from jax.experimental.pallas import tpu as pltpu

D_MODEL = 1024
GRID_W = 64
NA_HEADS = 8
NA_HEAD_DIM = 64
NA_WIN_H = 8
NA_WIN_W = 16
MLA_HEADS = 8
MLA_NOPE = 64
MLA_ROPE = 32
MLA_V = 64
Q_LORA = 384
KV_LORA = 256
ROPE_THETA = 10000.0
D_FF = 2816
PLE_DIM = 256
EPS = 1e-6
NA_WIDTH = NA_HEADS * NA_HEAD_DIM
MLA_WIDTH = MLA_HEADS * MLA_V

MLA_QK_PAD = 128
MLA_V_PAD = 80
NA_V_PAD = 80
LOG2_E = 1.4426950408889634
NA_GROUP_ROWS = 4
NA_KEY_ROWS = 12
NA_UNROLL = 32
NA_SCORE_BUFS = 4
NA_LOOKAHEAD = 3
NA_V_CHUNK = 256
MASK_VALUE = -1e30

TM_IN = 512
TM_POST = 512
MLA_TQ = 512
MLA_TK = 256
MLA_UNROLL = 64
MLA_SCORE_BUFS = 4
MLA_LOOKAHEAD = 2
VMEM_LIMIT_BYTES = 56 * 1024 * 1024

BF16 = jnp.bfloat16
F32 = jnp.float32


def _dot(a, b):
    return jnp.dot(a, b, preferred_element_type=F32)


def _dot_nt(a, b):
    return lax.dot_general(a, b, (((1,), (1,)), ((), ())), preferred_element_type=F32)


def _dot_tn(a, b):
    return lax.dot_general(a, b, (((0,), (0,)), ((), ())), preferred_element_type=F32)


def _rms(x, g):
    ms = jnp.mean(x * x, axis=-1, keepdims=True)
    return x * lax.rsqrt(ms + EPS) * g


def _const_spec(shape):
    nd = len(shape)
    return pl.BlockSpec(shape, lambda *_: (0,) * nd, pipeline_mode=pl.Buffered(1))


def _chunked_spec(rows, tm, chunk):
    if tm >= chunk:
        return pl.BlockSpec((1, tm // chunk, rows, chunk), lambda b, i: (b, i, 0, 0))
    sub = chunk // tm
    return pl.BlockSpec((1, 1, rows, tm), lambda b, i: (b, i // sub, 0, i % sub))


def _store_chunked(ref, r0, val):
    width = ref.shape[3]
    for c in range(ref.shape[1]):
        ref[0, c, r0:r0 + val.shape[0], :] = val if val.shape[1] == width else val[:, c * width:(c + 1) * width]


def _inproj_body(x_ref, gmix_ref, wkna_ref, wqnaT_ref, wvnaT_ref, wz_ref, wg_ref,
                 gq_ref, wuqT_ref, gkv_ref, wuk_ref, wuvT_ref,
                 cosT_ref, sinT_ref, cos_ref, sin_ref,
                 kna_ref, qTna_ref, vTna_ref, gates_ref, qT_ref, k_ref, vT_ref):
    tm = x_ref.shape[1]
    h = _rms(x_ref[0], gmix_ref[...]).astype(BF16)

    z = _dot(h, wz_ref[...])
    zqn = _rms(z[:, :Q_LORA], gq_ref[...]).astype(BF16)
    zkvn = _rms(z[:, Q_LORA:Q_LORA + KV_LORA], gkv_ref[...]).astype(BF16)
    zrr = z[:, Q_LORA + KV_LORA:]

    gates_ref[0] = jax.nn.sigmoid(_dot(h, wg_ref[...]))

    assert vT_ref.shape[3] == vTna_ref.shape[3]
    ones_rows = (lax.broadcasted_iota(jnp.int32, (NA_V_PAD - NA_HEAD_DIM, vT_ref.shape[3]), 0) == 0
                 ).astype(F32).astype(BF16)

    scale = (MLA_NOPE + MLA_ROPE) ** -0.5 * LOG2_E
    qT = _dot_nt(wuqT_ref[...], zqn)
    cT = cosT_ref[...]
    sT = sinT_ref[...]
    half = MLA_ROPE // 2
    for hh in range(MLA_HEADS):
        base = hh * MLA_QK_PAD
        nope = qT[base:base + MLA_NOPE]
        x1 = qT[base + MLA_NOPE:base + MLA_NOPE + half]
        x2 = qT[base + MLA_NOPE + half:base + MLA_NOPE + MLA_ROPE]
        _store_chunked(qT_ref, base, (nope * scale).astype(BF16))
        _store_chunked(qT_ref, base + MLA_NOPE, ((x1 * cT - x2 * sT) * scale).astype(BF16))
        _store_chunked(qT_ref, base + MLA_NOPE + half, ((x2 * cT + x1 * sT) * scale).astype(BF16))
        _store_chunked(qT_ref, base + MLA_NOPE + MLA_ROPE, jnp.zeros((MLA_QK_PAD - MLA_NOPE - MLA_ROPE, tm), BF16))

    k_rope = zrr[:, :MLA_ROPE] * cos_ref[...] + zrr[:, MLA_ROPE:] * sin_ref[...]
    k_cmp = _dot(zkvn, wuk_ref[...])
    pad = jnp.zeros((tm, MLA_QK_PAD - MLA_NOPE - MLA_ROPE), F32)
    pieces = []
    for hh in range(MLA_HEADS):
        pieces += [k_cmp[:, hh * MLA_NOPE:(hh + 1) * MLA_NOPE], k_rope, pad]
    k_ref[0] = jnp.concatenate(pieces, axis=1).astype(BF16)

    vT = _dot_nt(wuvT_ref[...], zkvn).astype(BF16)
    for hh in range(MLA_HEADS):
        _store_chunked(vT_ref, hh * MLA_V_PAD, vT[hh * MLA_V:(hh + 1) * MLA_V])
        _store_chunked(vT_ref, hh * MLA_V_PAD + MLA_V, ones_rows)

    kna_ref[0] = _dot(h, wkna_ref[...]).astype(BF16)
    _store_chunked(qTna_ref, 0, (_dot_nt(wqnaT_ref[...], h) * LOG2_E).astype(BF16))
    vTna = _dot_nt(wvnaT_ref[...], h).astype(BF16)
    for hh in range(NA_HEADS):
        _store_chunked(vTna_ref, hh * NA_V_PAD, vTna[hh * NA_HEAD_DIM:(hh + 1) * NA_HEAD_DIM])
        _store_chunked(vTna_ref, hh * NA_V_PAD + NA_HEAD_DIM, ones_rows)


def _inproj(x, w, tabs):
    B, S, D = x.shape
    tm = TM_IN
    nt = S // tm
    assert NA_V_PAD - NA_HEAD_DIM == MLA_V_PAD - MLA_V
    hq = MLA_HEADS * MLA_QK_PAD
    na_nq = NA_GROUP_ROWS * GRID_W
    weights = (w["g_mix"], w["w_kna"], w["w_qnaT"], w["w_vnaT"], w["w_z"], w["w_g"],
               w["g_q"], w["w_uqT"], w["g_kv"], w["w_uk"], w["w_uvT"])
    in_specs = [pl.BlockSpec((1, tm, D), lambda b, i: (b, i, 0))]
    in_specs += [_const_spec(a.shape) for a in weights]
    in_specs += [
        pl.BlockSpec((MLA_ROPE // 2, tm), lambda b, i: (0, i)),
        pl.BlockSpec((MLA_ROPE // 2, tm), lambda b, i: (0, i)),
        pl.BlockSpec((tm, MLA_ROPE), lambda b, i: (i, 0)),
        pl.BlockSpec((tm, MLA_ROPE), lambda b, i: (i, 0)),
    ]
    out_shape = (
        jax.ShapeDtypeStruct((B, S, NA_WIDTH), BF16),
        jax.ShapeDtypeStruct((B, S // na_nq, NA_WIDTH, na_nq), BF16),
        jax.ShapeDtypeStruct((B, S // NA_V_CHUNK, NA_HEADS * NA_V_PAD, NA_V_CHUNK), BF16),
        jax.ShapeDtypeStruct((B, S, 2 * D), F32),
        jax.ShapeDtypeStruct((B, S // MLA_TQ, hq, MLA_TQ), BF16),
        jax.ShapeDtypeStruct((B, S, hq), BF16),
        jax.ShapeDtypeStruct((B, S // MLA_TK, MLA_HEADS * MLA_V_PAD, MLA_TK), BF16),
    )
    out_specs = (
        pl.BlockSpec((1, tm, NA_WIDTH), lambda b, i: (b, i, 0)),
        _chunked_spec(NA_WIDTH, tm, na_nq),
        _chunked_spec(NA_HEADS * NA_V_PAD, tm, NA_V_CHUNK),
        pl.BlockSpec((1, tm, 2 * D), lambda b, i: (b, i, 0)),
        _chunked_spec(hq, tm, MLA_TQ),
        pl.BlockSpec((1, tm, hq), lambda b, i: (b, i, 0)),
        _chunked_spec(MLA_HEADS * MLA_V_PAD, tm, MLA_TK),
    )
    return pl.pallas_call(
        _inproj_body,
        grid=(B, nt),
        in_specs=in_specs,
        out_specs=out_specs,
        out_shape=out_shape,
        compiler_params=pltpu.CompilerParams(
            dimension_semantics=("arbitrary", "arbitrary"), vmem_limit_bytes=VMEM_LIMIT_BYTES),
        name="inproj",
    )(x, *weights, tabs["cosT"][:, :S], tabs["sinT"][:, :S], tabs["cos"][:S], tabs["sin"][:S])


def _na_body(qT_ref, k_ref, vT_ref, bias_ref, o_ref, *s_refs, rows, unroll):
    ng, nq = qT_ref.shape[1], qT_ref.shape[3]
    nk = NA_KEY_ROWS * GRID_W
    nbuf = len(s_refs)
    row = lax.broadcasted_iota(jnp.int32, (2 * NA_HEAD_DIM, nq), 0)

    def key_row0(g):
        return jnp.clip(NA_GROUP_ROWS * g - NA_WIN_H // 2, 0, rows - NA_KEY_ROWS)

    def scores_into(g, hh, s_ref):
        tok0 = pl.multiple_of(key_row0(g) * GRID_W, NA_V_CHUNK)
        kwin = k_ref[0, pl.ds(tok0, nk), :]
        qT = qT_ref[0, g]
        lo = hh * NA_HEAD_DIM
        q_h = jnp.where((row >= lo) & (row < lo + NA_HEAD_DIM), qT, jnp.zeros_like(qT))
        kind = jnp.where(g == 0, 0, jnp.where(g == ng - 1, 2, 1))
        s = _dot(kwin, q_h) + bias_ref[kind, hh]
        s_ref[...] = s
        return jnp.max(s, axis=0, keepdims=True)

    def softmax_pv(g, hh, s_ref, mx):
        c0 = lax.shift_right_logical(key_row0(g), 2)
        pb = jnp.exp2(s_ref[...] - mx).astype(BF16)
        acc = jnp.zeros((NA_V_PAD, nq), F32)
        for c in range(nk // NA_V_CHUNK):
            v_c = vT_ref[0, c0 + c, hh * NA_V_PAD:(hh + 1) * NA_V_PAD, :]
            acc = acc + _dot(v_c, pb[c * NA_V_CHUNK:(c + 1) * NA_V_CHUNK])
        o_ref[0, g, hh * NA_HEAD_DIM:(hh + 1) * NA_HEAD_DIM, :] = (
            acc[:NA_HEAD_DIM] / acc[NA_HEAD_DIM:NA_HEAD_DIM + 1]).astype(BF16)

    def trip(i, mxs):
        g0 = i * (unroll // 2)
        for u in range(unroll):
            ahead = u + NA_LOOKAHEAD
            g_ahead = jnp.minimum(g0 + ahead // 2, ng - 1)
            mx_new = scores_into(g_ahead, ahead % 2, s_refs[ahead % nbuf])
            softmax_pv(g0 + u // 2, u % 2, s_refs[u % nbuf], mxs[0])
            mxs = mxs[1:] + (mx_new,)
        return mxs

    mxs = tuple(scores_into(t // 2, t % 2, s_refs[t]) for t in range(NA_LOOKAHEAD))
    lax.fori_loop(0, 2 * ng // unroll, trip, mxs)


def _na(qT, k, vT, biasT):
    B, ng, _, nq = qT.shape
    S = k.shape[1]
    rows = S // GRID_W
    nk = NA_KEY_ROWS * GRID_W
    pair = 2 * NA_HEAD_DIM
    assert nq == NA_GROUP_ROWS * GRID_W and rows >= NA_KEY_ROWS
    unroll = next(u for u in range(NA_UNROLL, 0, -NA_SCORE_BUFS) if (2 * ng) % u == 0)
    assert NA_UNROLL % NA_SCORE_BUFS == 0 and NA_SCORE_BUFS % 2 == 0 and NA_LOOKAHEAD < NA_SCORE_BUFS <= unroll
    return pl.pallas_call(
        functools.partial(_na_body, rows=rows, unroll=unroll),
        grid=(B, NA_HEADS // 2),
        in_specs=[
            pl.BlockSpec((1, ng, pair, nq), lambda b, hp: (b, 0, hp, 0)),
            pl.BlockSpec((1, S, pair), lambda b, hp: (b, 0, hp)),
            pl.BlockSpec((1, S // NA_V_CHUNK, 2 * NA_V_PAD, NA_V_CHUNK), lambda b, hp: (b, 0, hp, 0)),
            pl.BlockSpec((3, 2, nk, nq), lambda b, hp: (0, hp, 0, 0)),
        ],
        out_specs=pl.BlockSpec((1, ng, pair, nq), lambda b, hp: (b, 0, hp, 0)),
        out_shape=jax.ShapeDtypeStruct((B, ng, NA_WIDTH, nq), BF16),
        scratch_shapes=[pltpu.VMEM((nk, nq), F32)] * NA_SCORE_BUFS,
        compiler_params=pltpu.CompilerParams(
            dimension_semantics=("arbitrary", "arbitrary"), vmem_limit_bytes=VMEM_LIMIT_BYTES),
        name="natten",
    )(qT, k, vT, biasT)


def _na_bias_table(rpb):
    n_off_r, n_off_c = 2 * NA_WIN_H - 1, 2 * NA_WIN_W - 1
    cols = np.arange(GRID_W)
    cs = np.clip(cols - NA_WIN_W // 2, 0, GRID_W - NA_WIN_W)
    col_ok = (cols[:, None] >= cs[None, :]) & (cols[:, None] < cs[None, :] + NA_WIN_W)
    col_idx = np.clip(cols[:, None] - cols[None, :] + NA_WIN_W - 1, 0, n_off_c - 1)
    onehot = (col_idx[None] == np.arange(n_off_c)[:, None, None]) & col_ok[None]
    onehot = jnp.asarray(onehot.reshape(n_off_c, GRID_W * GRID_W), F32)
    blocks = jnp.dot(rpb.reshape(NA_HEADS * n_off_r, n_off_c), onehot, precision=lax.Precision.HIGHEST)
    blocks = blocks.reshape(NA_HEADS, n_off_r, GRID_W, GRID_W)
    blocks = jnp.where(jnp.asarray(col_ok)[None, None], blocks * LOG2_E, MASK_VALUE)
    masked = jnp.full((NA_HEADS, 1, GRID_W, GRID_W), MASK_VALUE, F32)
    blocks = jnp.concatenate([blocks, masked], axis=1)

    rows = 4 * NA_KEY_ROWS
    kinds = ((0, 0), (NA_KEY_ROWS, NA_KEY_ROWS - NA_WIN_H // 2), (rows - NA_GROUP_ROWS, rows - NA_KEY_ROWS))
    sel = np.zeros((3, NA_KEY_ROWS, NA_GROUP_ROWS), np.int32)
    for t, (r0, ws) in enumerate(kinds):
        q_row = r0 + np.arange(NA_GROUP_ROWS)
        k_row = ws + np.arange(NA_KEY_ROWS)
        rs = np.clip(q_row - NA_WIN_H // 2, 0, rows - NA_WIN_H)
        row_ok = (k_row[:, None] >= rs[None, :]) & (k_row[:, None] < rs[None, :] + NA_WIN_H)
        sel[t] = np.where(row_ok, k_row[:, None] - q_row[None, :] + NA_WIN_H - 1, n_off_r)
    table = jnp.take(blocks, jnp.asarray(sel.reshape(-1)), axis=1)
    table = table.reshape(NA_HEADS, 3, NA_KEY_ROWS, NA_GROUP_ROWS, GRID_W, GRID_W)
    table = table.transpose(1, 0, 2, 4, 3, 5)
    return table.reshape(3, NA_HEADS, NA_KEY_ROWS * GRID_W, NA_GROUP_ROWS * GRID_W)


def _mla_body(qT_ref, k_ref, vT_ref, o_ref, *s_refs, unroll):
    nq, tq = qT_ref.shape[1], qT_ref.shape[3]
    nchunk, tk = vT_ref.shape[1], vT_ref.shape[3]
    nbuf = len(s_refs)

    def scores_into(qb, j, s_ref):
        s = _dot(k_ref[0, j * tk:(j + 1) * tk, :], qT_ref[0, qb])
        s_ref[...] = s
        return jnp.max(s, axis=0, keepdims=True)

    def softmax_pv(j, s_ref, mx, m, acc):
        m_new = jnp.maximum(m, mx)
        alpha = jnp.exp2(m - m_new)
        p = jnp.exp2(s_ref[...] - m_new).astype(BF16)
        return m_new, alpha * acc + _dot(vT_ref[0, j], p)

    blocks_per_trip = unroll // nchunk

    def trip(i, mxs):
        qb0 = i * blocks_per_trip
        qb1 = jnp.minimum(qb0 + blocks_per_trip, nq - blocks_per_trip)
        for u in range(unroll):
            if u % nchunk == 0:
                m, acc = jnp.full((1, tq), -jnp.inf, F32), jnp.zeros((MLA_V_PAD, tq), F32)
            ahead = u + MLA_LOOKAHEAD
            qb_ahead = (qb0 if ahead < unroll else qb1) + (ahead % unroll) // nchunk
            mx_new = scores_into(qb_ahead, ahead % nchunk, s_refs[ahead % nbuf])
            m, acc = softmax_pv(u % nchunk, s_refs[u % nbuf], mxs[0], m, acc)
            mxs = mxs[1:] + (mx_new,)
            if (u + 1) % nchunk == 0:
                o_ref[0, qb0 + u // nchunk] = (acc[:MLA_V] / acc[MLA_V:MLA_V + 1]).astype(BF16)
        return mxs

    mxs = tuple(scores_into(0, t, s_refs[t]) for t in range(MLA_LOOKAHEAD))
    lax.fori_loop(0, nq // blocks_per_trip, trip, mxs)


def _mla(qT, k, vT):
    B, nq, _, tq = qT.shape
    S = k.shape[1]
    nchunk, tk = vT.shape[1], vT.shape[3]
    unroll = next(u for u in range(MLA_UNROLL, 0, -MLA_SCORE_BUFS)
                  if u % nchunk == 0 and nq % (u // nchunk) == 0)
    assert MLA_UNROLL % MLA_SCORE_BUFS == 0 and MLA_LOOKAHEAD < MLA_SCORE_BUFS <= unroll
    return pl.pallas_call(
        functools.partial(_mla_body, unroll=unroll),
        grid=(B, MLA_HEADS),
        in_specs=[
            pl.BlockSpec((1, nq, MLA_QK_PAD, tq), lambda b, h: (b, 0, h, 0)),
            pl.BlockSpec((1, S, MLA_QK_PAD), lambda b, h: (b, 0, h)),
            pl.BlockSpec((1, nchunk, MLA_V_PAD, tk), lambda b, h: (b, 0, h, 0)),
        ],
        out_specs=pl.BlockSpec((1, nq, MLA_V, tq), lambda b, h: (b, 0, h, 0)),
        out_shape=jax.ShapeDtypeStruct((B, nq, MLA_WIDTH, tq), BF16),
        scratch_shapes=[pltpu.VMEM((tk, tq), F32)] * MLA_SCORE_BUFS,
        compiler_params=pltpu.CompilerParams(
            dimension_semantics=("arbitrary", "arbitrary"), vmem_limit_bytes=VMEM_LIMIT_BYTES),
        name="mla",
    )(qT, k, vT)


def _post_body(x_ref, oTna_ref, oTmla_ref, gates_ref, p_ref,
               wnao_ref, wmlao_ref, wout_ref, gffn_ref, wgate_ref, wup_ref, wdown_ref,
               gple_ref, wpleg_ref, wple_ref, gfin_ref, y_ref):
    d = x_ref.shape[2]
    oTna = jnp.concatenate([oTna_ref[0, c] for c in range(oTna_ref.shape[1])], axis=1)
    a = _dot_tn(oTna, wnao_ref[...])
    oTmla = jnp.concatenate([oTmla_ref[0, c] for c in range(oTmla_ref.shape[1])], axis=1)
    b = _dot_tn(oTmla, wmlao_ref[...])
    gates = gates_ref[0]
    mix = (gates[:, :d] * a + gates[:, d:] * b).astype(BF16)
    x1 = x_ref[0] + _dot(mix, wout_ref[...])
    ple = _dot(p_ref[0].astype(BF16), wple_ref[...])

    hn = _rms(x1, gffn_ref[...]).astype(BF16)
    gate = _dot(hn, wgate_ref[...])
    up = _dot(hn, wup_ref[...])
    act = (gate * jax.nn.sigmoid(gate) * up).astype(BF16)
    x2 = x1 + _dot(act, wdown_ref[...])

    ple_gate = jax.nn.sigmoid(_dot(_rms(x2, gple_ref[...]).astype(BF16), wpleg_ref[...]))
    x3 = x2 + ple_gate * ple
    y_ref[0] = _rms(x3, gfin_ref[...])


def _post(x, oTna, oTmla, gates, p, w):
    B, S, D = x.shape
    tm = TM_POST
    weights = (w["w_na_o"], w["w_mla_o"], w["w_out"], w["g_ffn"], w["w_gate"], w["w_up"], w["w_down"],
               w["g_ple"], w["w_ple_gate"], w["w_ple"], w["g_final"])
    in_specs = [
        pl.BlockSpec((1, tm, D), lambda b, i: (b, i, 0)),
        _chunked_spec(NA_WIDTH, tm, NA_GROUP_ROWS * GRID_W),
        _chunked_spec(MLA_WIDTH, tm, MLA_TQ),
        pl.BlockSpec((1, tm, 2 * D), lambda b, i: (b, i, 0)),
        pl.BlockSpec((1, tm, PLE_DIM), lambda b, i: (b, i, 0)),
    ]
    in_specs += [_const_spec(a.shape) for a in weights]
    return pl.pallas_call(
        _post_body,
        grid=(B, S // tm),
        in_specs=in_specs,
        out_specs=pl.BlockSpec((1, tm, D), lambda b, i: (b, i, 0)),
        out_shape=jax.ShapeDtypeStruct((B, S, D), F32),
        compiler_params=pltpu.CompilerParams(
            dimension_semantics=("arbitrary", "arbitrary"), vmem_limit_bytes=VMEM_LIMIT_BYTES),
        name="post",
    )(x, oTna, oTmla, gates, p, *weights)


def _prep_weights(g_mix, w_in, g_q_lat, w_uq, g_kv_lat, w_ukv, w_na_o, w_mla_o, w_out,
                  g_ffn, w_gate_up, w_down, g_ple, w_ple_gate, w_ple, g_final):
    o = np.cumsum((0, NA_WIDTH, NA_WIDTH, NA_WIDTH, Q_LORA, KV_LORA, MLA_ROPE, 2 * D_MODEL))
    w_q, w_k, w_v, w_zq, w_zkv, w_zr, w_g = (w_in[:, o[i]:o[i + 1]] for i in range(7))
    half = MLA_ROPE // 2
    w_zr_rot = jnp.concatenate([-w_zr[:, half:], w_zr[:, :half]], axis=1)

    w_uq_h = w_uq.reshape(Q_LORA, MLA_HEADS, MLA_NOPE + MLA_ROPE)
    w_uq_pad = jnp.pad(w_uq_h, ((0, 0), (0, 0), (0, MLA_QK_PAD - MLA_NOPE - MLA_ROPE)))
    w_uqT = w_uq_pad.reshape(Q_LORA, MLA_HEADS * MLA_QK_PAD).T

    w_ukv_h = w_ukv.reshape(KV_LORA, MLA_HEADS, MLA_NOPE + MLA_V)
    w_uk = w_ukv_h[:, :, :MLA_NOPE].reshape(KV_LORA, MLA_HEADS * MLA_NOPE)
    w_uvT = w_ukv_h[:, :, MLA_NOPE:].reshape(KV_LORA, MLA_WIDTH).T

    w_gate = w_gate_up[:, :D_FF]
    w_up = w_gate_up[:, D_FF:]
    w_dn = w_down

    row = lambda g: g.reshape(1, -1).astype(F32)
    return {
        "g_mix": row(g_mix),
        "w_kna": w_k.astype(BF16),
        "w_qnaT": (w_q * NA_HEAD_DIM ** -0.5).T.astype(BF16),
        "w_vnaT": w_v.T.astype(BF16),
        "w_z": jnp.concatenate([w_zq, w_zkv, w_zr, w_zr_rot], axis=1).astype(BF16),
        "w_g": w_g.astype(BF16),
        "g_q": row(g_q_lat),
        "w_uqT": w_uqT.astype(BF16),
        "g_kv": row(g_kv_lat),
        "w_uk": w_uk.astype(BF16),
        "w_uvT": w_uvT.astype(BF16),
        "w_na_o": w_na_o.astype(BF16),
        "w_mla_o": w_mla_o.astype(BF16),
        "w_out": w_out.astype(BF16),
        "g_ffn": row(g_ffn),
        "w_gate": w_gate.astype(BF16),
        "w_up": w_up.astype(BF16),
        "w_down": w_dn.astype(BF16),
        "g_ple": row(g_ple),
        "w_ple_gate": w_ple_gate.astype(BF16),
        "w_ple": w_ple.astype(BF16),
        "g_final": row(g_final),
    }


def _rope_tables(s_max):
    half = MLA_ROPE // 2
    pos = jnp.arange(s_max, dtype=F32)
    freqs = ROPE_THETA ** (-jnp.arange(half, dtype=F32) / half)
    ang = pos[:, None] * freqs[None, :]
    cos, sin = jnp.cos(ang), jnp.sin(ang)
    return {"cosT": cos.T, "sinT": sin.T,
            "cos": jnp.concatenate([cos, cos], axis=1), "sin": jnp.concatenate([sin, sin], axis=1)}


def _trunk(x, p, w, tabs, biasT):
    kna, qTna, vTna, gates, qT, k, vT = _inproj(x, w, tabs)
    oTna = _na(qTna, kna, vTna, biasT)
    oTmla = _mla(qT, k, vT)
    return _post(x, oTna, oTmla, gates, p, w)


def kernel(x_prompt, x_sample, p_prompt, p_sample, g_mix, w_in, rpb, g_q_lat, w_uq, g_kv_lat, w_ukv,
           w_na_o, w_mla_o, w_out, g_ffn, w_gate_up, w_down, g_ple, w_ple_gate, w_ple, g_final):
    assert g_mix.shape[0] == 1, "single-layer trunk"
    w = _prep_weights(g_mix[0], w_in[0], g_q_lat[0], w_uq[0], g_kv_lat[0], w_ukv[0], w_na_o[0], w_mla_o[0],
                      w_out[0], g_ffn[0], w_gate_up[0], w_down[0], g_ple[0], w_ple_gate[0], w_ple[0], g_final)
    biasT = _na_bias_table(rpb[0])
    tabs = _rope_tables(max(x_prompt.shape[1], x_sample.shape[1]))
    y_prompt = _trunk(x_prompt, p_prompt[0], w, tabs, biasT)
    y_sample = _trunk(x_sample, p_sample[0], w, tabs, biasT)
    return (y_prompt, y_sample)
```

```python
import functools

import numpy as np
import jax
import jax.numpy as jnp
from jax import lax
from jax.experimental import pallas as pl
from jax.experimental.pallas import tpu as pltpu

D_MODEL = 1024
GRID_W = 64
NA_HEADS = 8
NA_HEAD_DIM = 64
NA_WIN_H = 8
NA_WIN_W = 16
MLA_HEADS = 8
MLA_NOPE = 64
MLA_ROPE = 32
MLA_V = 64
Q_LORA = 384
KV_LORA = 256
ROPE_THETA = 10000.0
D_FF = 2816
PLE_DIM = 256
EPS = 1e-6
NA_WIDTH = NA_HEADS * NA_HEAD_DIM
MLA_WIDTH = MLA_HEADS * MLA_V

MLA_QK_PAD = 128
MLA_V_PAD = 80
NA_V_PAD = 80
LOG2_E = 1.4426950408889634
NA_GROUP_ROWS = 4
NA_KEY_ROWS = 12
NA_UNROLL = 32
NA_SCORE_BUFS = 4
NA_LOOKAHEAD = 3
NA_V_CHUNK = 256
MASK_VALUE = -1e30

TM_IN = 512
TM_POST = 512
MLA_TQ = 512
MLA_TK = 256
MLA_UNROLL = 64
MLA_SCORE_BUFS = 4
MLA_LOOKAHEAD = 2
VMEM_LIMIT_BYTES = 56 * 1024 * 1024

BF16 = jnp.bfloat16
F32 = jnp.float32


def _dot(a, b):
    return jnp.dot(a, b, preferred_element_type=F32)


def _dot_nt(a, b):
    return lax.dot_general(a, b, (((1,), (1,)), ((), ())), preferred_element_type=F32)


def _dot_tn(a, b):
    return lax.dot_general(a, b, (((0,), (0,)), ((), ())), preferred_element_type=F32)


def _rms(x, g):
    ms = jnp.mean(x * x, axis=-1, keepdims=True)
    return x * lax.rsqrt(ms + EPS) * g


def _const_spec(shape):
    nd = len(shape)
    return pl.BlockSpec(shape, lambda *_: (0,) * nd, pipeline_mode=pl.Buffered(1))


def _chunked_spec(rows, tm, chunk):
    if tm >= chunk:
        return pl.BlockSpec((1, tm // chunk, rows, chunk), lambda b, i: (b, i, 0, 0))
    sub = chunk // tm
    return pl.BlockSpec((1, 1, rows, tm), lambda b, i: (b, i // sub, 0, i % sub))


def _store_chunked(ref, r0, val):
    width = ref.shape[3]
    for c in range(ref.shape[1]):
        ref[0, c, r0:r0 + val.shape[0], :] = val if val.shape[1] == width else val[:, c * width:(c + 1) * width]


def _inproj_body(x_ref, gmix_ref, wkna_ref, wqnaT_ref, wvnaT_ref, wz_ref, wg_ref,
                 gq_ref, wuqT_ref, gkv_ref, wuk_ref, wuvT_ref,
                 cosT_ref, sinT_ref, cos_ref, sin_ref,
                 kna_ref, qTna_ref, vTna_ref, gates_ref, qT_ref, k_ref, vT_ref):
    tm = x_ref.shape[1]
    h = _rms(x_ref[0], gmix_ref[...]).astype(BF16)

    z = _dot(h, wz_ref[...])
    zqn = _rms(z[:, :Q_LORA], gq_ref[...]).astype(BF16)
    zkvn = _rms(z[:, Q_LORA:Q_LORA + KV_LORA], gkv_ref[...]).astype(BF16)
    zrr = z[:, Q_LORA + KV_LORA:]

    gates_ref[0] = jax.nn.sigmoid(_dot(h, wg_ref[...]))

    assert vT_ref.shape[3] == vTna_ref.shape[3]
    ones_rows = (lax.broadcasted_iota(jnp.int32, (NA_V_PAD - NA_HEAD_DIM, vT_ref.shape[3]), 0) == 0
                 ).astype(F32).astype(BF16)

    scale = (MLA_NOPE + MLA_ROPE) ** -0.5 * LOG2_E
    qT = _dot_nt(wuqT_ref[...], zqn)
    cT = cosT_ref[...]
    sT = sinT_ref[...]
    half = MLA_ROPE // 2
    for hh in range(MLA_HEADS):
        src = hh * (MLA_NOPE + MLA_ROPE)
        base = hh * MLA_QK_PAD
        nope = qT[src:src + MLA_NOPE]
        x1 = qT[src + MLA_NOPE:src + MLA_NOPE + half]
        x2 = qT[src + MLA_NOPE + half:src + MLA_NOPE + MLA_ROPE]
        _store_chunked(qT_ref, base, (nope * scale).astype(BF16))
        _store_chunked(qT_ref, base + MLA_NOPE, ((x1 * cT - x2 * sT) * scale).astype(BF16))
        _store_chunked(qT_ref, base + MLA_NOPE + half, ((x2 * cT + x1 * sT) * scale).astype(BF16))
        _store_chunked(qT_ref, base + MLA_NOPE + MLA_ROPE, jnp.zeros((MLA_QK_PAD - MLA_NOPE - MLA_ROPE, tm), BF16))

    k_rope = zrr[:, :MLA_ROPE] * cos_ref[...] + zrr[:, MLA_ROPE:] * sin_ref[...]
    k_cmp = _dot(zkvn, wuk_ref[...])
    pad = jnp.zeros((tm, MLA_QK_PAD - MLA_NOPE - MLA_ROPE), F32)
    pieces = []
    for hh in range(MLA_HEADS):
        pieces += [k_cmp[:, hh * MLA_NOPE:(hh + 1) * MLA_NOPE], k_rope, pad]
    k_ref[0] = jnp.concatenate(pieces, axis=1).astype(BF16)

    vT = _dot_nt(wuvT_ref[...], zkvn).astype(BF16)
    for hh in range(MLA_HEADS):
        _store_chunked(vT_ref, hh * MLA_V_PAD, vT[hh * MLA_V:(hh + 1) * MLA_V])
        _store_chunked(vT_ref, hh * MLA_V_PAD + MLA_V, ones_rows)

    kna_ref[0] = _dot(h, wkna_ref[...]).astype(BF16)
    _store_chunked(qTna_ref, 0, (_dot_nt(wqnaT_ref[...], h) * LOG2_E).astype(BF16))
    vTna = _dot_nt(wvnaT_ref[...], h).astype(BF16)
    for hh in range(NA_HEADS):
        _store_chunked(vTna_ref, hh * NA_V_PAD, vTna[hh * NA_HEAD_DIM:(hh + 1) * NA_HEAD_DIM])
        _store_chunked(vTna_ref, hh * NA_V_PAD + NA_HEAD_DIM, ones_rows)


def _inproj(x, w, tabs):
    B, S, D = x.shape
    tm = TM_IN
    nt = S // tm
    assert NA_V_PAD - NA_HEAD_DIM == MLA_V_PAD - MLA_V
    hq = MLA_HEADS * MLA_QK_PAD
    na_nq = NA_GROUP_ROWS * GRID_W
    weights = (w["g_mix"], w["w_kna"], w["w_qnaT"], w["w_vnaT"], w["w_z"], w["w_g"],
               w["g_q"], w["w_uqT"], w["g_kv"], w["w_uk"], w["w_uvT"])
    in_specs = [pl.BlockSpec((1, tm, D), lambda b, i: (b, i, 0))]
    in_specs += [_const_spec(a.shape) for a in weights]
    in_specs += [
        pl.BlockSpec((MLA_ROPE // 2, tm), lambda b, i: (0, i)),
        pl.BlockSpec((MLA_ROPE // 2, tm), lambda b, i: (0, i)),
        pl.BlockSpec((tm, MLA_ROPE), lambda b, i: (i, 0)),
        pl.BlockSpec((tm, MLA_ROPE), lambda b, i: (i, 0)),
    ]
    out_shape = (
        jax.ShapeDtypeStruct((B, S, NA_WIDTH), BF16),
        jax.ShapeDtypeStruct((B, S // na_nq, NA_WIDTH, na_nq), BF16),
        jax.ShapeDtypeStruct((B, S // NA_V_CHUNK, NA_HEADS * NA_V_PAD, NA_V_CHUNK), BF16),
        jax.ShapeDtypeStruct((B, S, 2 * D), F32),
        jax.ShapeDtypeStruct((B, S // MLA_TQ, hq, MLA_TQ), BF16),
        jax.ShapeDtypeStruct((B, S, hq), BF16),
        jax.ShapeDtypeStruct((B, S // MLA_TK, MLA_HEADS * MLA_V_PAD, MLA_TK), BF16),
    )
    out_specs = (
        pl.BlockSpec((1, tm, NA_WIDTH), lambda b, i: (b, i, 0)),
        _chunked_spec(NA_WIDTH, tm, na_nq),
        _chunked_spec(NA_HEADS * NA_V_PAD, tm, NA_V_CHUNK),
        pl.BlockSpec((1, tm, 2 * D), lambda b, i: (b, i, 0)),
        _chunked_spec(hq, tm, MLA_TQ),
        pl.BlockSpec((1, tm, hq), lambda b, i: (b, i, 0)),
        _chunked_spec(MLA_HEADS * MLA_V_PAD, tm, MLA_TK),
    )
    return pl.pallas_call(
        _inproj_body,
        grid=(B, nt),
        in_specs=in_specs,
        out_specs=out_specs,
        out_shape=out_shape,
        compiler_params=pltpu.CompilerParams(
            dimension_semantics=("arbitrary", "arbitrary"), vmem_limit_bytes=VMEM_LIMIT_BYTES),
        name="inproj",
    )(x, *weights, tabs["cosT"][:, :S], tabs["sinT"][:, :S], tabs["cos"][:S], tabs["sin"][:S])


def _na_body(qT_ref, k_ref, vT_ref, bias_ref, o_ref, *s_refs, rows, unroll):
    ng, nq = qT_ref.shape[1], qT_ref.shape[3]
    nk = NA_KEY_ROWS * GRID_W
    nbuf = len(s_refs)
    row = lax.broadcasted_iota(jnp.int32, (2 * NA_HEAD_DIM, nq), 0)

    def key_row0(g):
        return jnp.clip(NA_GROUP_ROWS * g - NA_WIN_H // 2, 0, rows - NA_KEY_ROWS)

    def scores_into(g, hh, s_ref):
        tok0 = pl.multiple_of(key_row0(g) * GRID_W, NA_V_CHUNK)
        kwin = k_ref[0, pl.ds(tok0, nk), :]
        qT = qT_ref[0, g]
        lo = hh * NA_HEAD_DIM
        q_h = jnp.where((row >= lo) & (row < lo + NA_HEAD_DIM), qT, jnp.zeros_like(qT))
        kind = jnp.where(g == 0, 0, jnp.where(g == ng - 1, 2, 1))
        s = _dot(kwin, q_h) + bias_ref[kind, hh]
        s_ref[...] = s
        return jnp.max(s, axis=0, keepdims=True)

    def softmax_pv(g, hh, s_ref, mx):
        c0 = lax.shift_right_logical(key_row0(g), 2)
        pb = jnp.exp2(s_ref[...] - mx).astype(BF16)
        acc = jnp.zeros((NA_V_PAD, nq), F32)
        for c in range(nk // NA_V_CHUNK):
            v_c = vT_ref[0, c0 + c, hh * NA_V_PAD:(hh + 1) * NA_V_PAD, :]
            acc = acc + _dot(v_c, pb[c * NA_V_CHUNK:(c + 1) * NA_V_CHUNK])
        o_ref[0, g, hh * NA_HEAD_DIM:(hh + 1) * NA_HEAD_DIM, :] = (
            acc[:NA_HEAD_DIM] / acc[NA_HEAD_DIM:NA_HEAD_DIM + 1]).astype(BF16)

    def trip(i, mxs):
        g0 = i * (unroll // 2)
        for u in range(unroll):
            ahead = u + NA_LOOKAHEAD
            g_ahead = jnp.minimum(g0 + ahead // 2, ng - 1)
            mx_new = scores_into(g_ahead, ahead % 2, s_refs[ahead % nbuf])
            softmax_pv(g0 + u // 2, u % 2, s_refs[u % nbuf], mxs[0])
            mxs = mxs[1:] + (mx_new,)
        return mxs

    mxs = tuple(scores_into(t // 2, t % 2, s_refs[t]) for t in range(NA_LOOKAHEAD))
    lax.fori_loop(0, 2 * ng // unroll, trip, mxs)


def _na(qT, k, vT, biasT):
    B, ng, _, nq = qT.shape
    S = k.shape[1]
    rows = S // GRID_W
    nk = NA_KEY_ROWS * GRID_W
    pair = 2 * NA_HEAD_DIM
    assert nq == NA_GROUP_ROWS * GRID_W and rows >= NA_KEY_ROWS
    unroll = next(u for u in range(NA_UNROLL, 0, -NA_SCORE_BUFS) if (2 * ng) % u == 0)
    assert NA_UNROLL % NA_SCORE_BUFS == 0 and NA_SCORE_BUFS % 2 == 0 and NA_LOOKAHEAD < NA_SCORE_BUFS <= unroll
    return pl.pallas_call(
        functools.partial(_na_body, rows=rows, unroll=unroll),
        grid=(B, NA_HEADS // 2),
        in_specs=[
            pl.BlockSpec((1, ng, pair, nq), lambda b, hp: (b, 0, hp, 0)),
            pl.BlockSpec((1, S, pair), lambda b, hp: (b, 0, hp)),
            pl.BlockSpec((1, S // NA_V_CHUNK, 2 * NA_V_PAD, NA_V_CHUNK), lambda b, hp: (b, 0, hp, 0)),
            pl.BlockSpec((3, 2, nk, nq), lambda b, hp: (0, hp, 0, 0)),
        ],
        out_specs=pl.BlockSpec((1, ng, pair, nq), lambda b, hp: (b, 0, hp, 0)),
        out_shape=jax.ShapeDtypeStruct((B, ng, NA_WIDTH, nq), BF16),
        scratch_shapes=[pltpu.VMEM((nk, nq), F32)] * NA_SCORE_BUFS,
        compiler_params=pltpu.CompilerParams(
            dimension_semantics=("arbitrary", "arbitrary"), vmem_limit_bytes=VMEM_LIMIT_BYTES),
        name="natten",
    )(qT, k, vT, biasT)


def _na_bias_table(rpb):
    n_off_r, n_off_c = 2 * NA_WIN_H - 1, 2 * NA_WIN_W - 1
    cols = np.arange(GRID_W)
    cs = np.clip(cols - NA_WIN_W // 2, 0, GRID_W - NA_WIN_W)
    col_ok = (cols[:, None] >= cs[None, :]) & (cols[:, None] < cs[None, :] + NA_WIN_W)
    col_idx = np.clip(cols[:, None] - cols[None, :] + NA_WIN_W - 1, 0, n_off_c - 1)
    onehot = (col_idx[None] == np.arange(n_off_c)[:, None, None]) & col_ok[None]
    onehot = jnp.asarray(onehot.reshape(n_off_c, GRID_W * GRID_W), F32)
    blocks = jnp.dot(rpb.reshape(NA_HEADS * n_off_r, n_off_c), onehot, precision=lax.Precision.HIGHEST)
    blocks = blocks.reshape(NA_HEADS, n_off_r, GRID_W, GRID_W)
    blocks = jnp.where(jnp.asarray(col_ok)[None, None], blocks * LOG2_E, MASK_VALUE)
    masked = jnp.full((NA_HEADS, 1, GRID_W, GRID_W), MASK_VALUE, F32)
    blocks = jnp.concatenate([blocks, masked], axis=1)

    rows = 4 * NA_KEY_ROWS
    kinds = ((0, 0), (NA_KEY_ROWS, NA_KEY_ROWS - NA_WIN_H // 2), (rows - NA_GROUP_ROWS, rows - NA_KEY_ROWS))
    sel = np.zeros((3, NA_KEY_ROWS, NA_GROUP_ROWS), np.int32)
    for t, (r0, ws) in enumerate(kinds):
        q_row = r0 + np.arange(NA_GROUP_ROWS)
        k_row = ws + np.arange(NA_KEY_ROWS)
        rs = np.clip(q_row - NA_WIN_H // 2, 0, rows - NA_WIN_H)
        row_ok = (k_row[:, None] >= rs[None, :]) & (k_row[:, None] < rs[None, :] + NA_WIN_H)
        sel[t] = np.where(row_ok, k_row[:, None] - q_row[None, :] + NA_WIN_H - 1, n_off_r)
    table = jnp.take(blocks, jnp.asarray(sel.reshape(-1)), axis=1)
    table = table.reshape(NA_HEADS, 3, NA_KEY_ROWS, NA_GROUP_ROWS, GRID_W, GRID_W)
    table = table.transpose(1, 0, 2, 4, 3, 5)
    return table.reshape(3, NA_HEADS, NA_KEY_ROWS * GRID_W, NA_GROUP_ROWS * GRID_W)


def _mla_body(qT_ref, k_ref, vT_ref, o_ref, *s_refs, unroll):
    nq, tq = qT_ref.shape[1], qT_ref.shape[3]
    nchunk, tk = vT_ref.shape[1], vT_ref.shape[3]
    nbuf = len(s_refs)

    def scores_into(qb, j, s_ref):
        s = _dot(k_ref[0, j * tk:(j + 1) * tk, :], qT_ref[0, qb])
        s_ref[...] = s
        return jnp.max(s, axis=0, keepdims=True)

    def softmax_pv(j, s_ref, mx, m, acc):
        m_new = jnp.maximum(m, mx)
        alpha = jnp.exp2(m - m_new)
        p = jnp.exp2(s_ref[...] - m_new).astype(BF16)
        return m_new, alpha * acc + _dot(vT_ref[0, j], p)

    blocks_per_trip = unroll // nchunk

    def trip(i, mxs):
        qb0 = i * blocks_per_trip
        qb1 = jnp.minimum(qb0 + blocks_per_trip, nq - blocks_per_trip)
        for u in range(unroll):
            if u % nchunk == 0:
                m, acc = jnp.full((1, tq), -jnp.inf, F32), jnp.zeros((MLA_V_PAD, tq), F32)
            ahead = u + MLA_LOOKAHEAD
            qb_ahead = (qb0 if ahead < unroll else qb1) + (ahead % unroll) // nchunk
            mx_new = scores_into(qb_ahead, ahead % nchunk, s_refs[ahead % nbuf])
            m, acc = softmax_pv(u % nchunk, s_refs[u % nbuf], mxs[0], m, acc)
            mxs = mxs[1:] + (mx_new,)
            if (u + 1) % nchunk == 0:
                o_ref[0, qb0 + u // nchunk] = (acc[:MLA_V] / acc[MLA_V:MLA_V + 1]).astype(BF16)
        return mxs

    mxs = tuple(scores_into(0, t, s_refs[t]) for t in range(MLA_LOOKAHEAD))
    lax.fori_loop(0, nq // blocks_per_trip, trip, mxs)


def _mla(qT, k, vT):
    B, nq, _, tq = qT.shape
    S = k.shape[1]
    nchunk, tk = vT.shape[1], vT.shape[3]
    unroll = next(u for u in range(MLA_UNROLL, 0, -MLA_SCORE_BUFS)
                  if u % nchunk == 0 and nq % (u // nchunk) == 0)
    assert MLA_UNROLL % MLA_SCORE_BUFS == 0 and MLA_LOOKAHEAD < MLA_SCORE_BUFS <= unroll
    return pl.pallas_call(
        functools.partial(_mla_body, unroll=unroll),
        grid=(B, MLA_HEADS),
        in_specs=[
            pl.BlockSpec((1, nq, MLA_QK_PAD, tq), lambda b, h: (b, 0, h, 0)),
            pl.BlockSpec((1, S, MLA_QK_PAD), lambda b, h: (b, 0, h)),
            pl.BlockSpec((1, nchunk, MLA_V_PAD, tk), lambda b, h: (b, 0, h, 0)),
        ],
        out_specs=pl.BlockSpec((1, nq, MLA_V, tq), lambda b, h: (b, 0, h, 0)),
        out_shape=jax.ShapeDtypeStruct((B, nq, MLA_WIDTH, tq), BF16),
        scratch_shapes=[pltpu.VMEM((tk, tq), F32)] * MLA_SCORE_BUFS,
        compiler_params=pltpu.CompilerParams(
            dimension_semantics=("arbitrary", "arbitrary"), vmem_limit_bytes=VMEM_LIMIT_BYTES),
        name="mla",
    )(qT, k, vT)


def _post_body(x_ref, oTna_ref, oTmla_ref, gates_ref, p_ref,
               wnao_ref, wmlao_ref, wout_ref, gffn_ref, wgate_ref, wup_ref, wdown_ref,
               gple_ref, wpleg_ref, wple_ref, gfin_ref, y_ref):
    d = x_ref.shape[2]
    oTna = jnp.concatenate([oTna_ref[0, c] for c in range(oTna_ref.shape[1])], axis=1)
    a = _dot_tn(oTna, wnao_ref[...])
    oTmla = jnp.concatenate([oTmla_ref[0, c] for c in range(oTmla_ref.shape[1])], axis=1)
    b = _dot_tn(oTmla, wmlao_ref[...])
    gates = gates_ref[0]
    mix = (gates[:, :d] * a + gates[:, d:] * b).astype(BF16)
    x1 = x_ref[0] + _dot(mix, wout_ref[...])
    ple = _dot(p_ref[0].astype(BF16), wple_ref[...])

    hn = _rms(x1, gffn_ref[...]).astype(BF16)
    gate = _dot(hn, wgate_ref[...])
    up = _dot(hn, wup_ref[...])
    act = (gate * jax.nn.sigmoid(gate) * up).astype(BF16)
    x2 = x1 + _dot(act, wdown_ref[...])

    ple_gate = jax.nn.sigmoid(_dot(_rms(x2, gple_ref[...]).astype(BF16), wpleg_ref[...]))
    x3 = x2 + ple_gate * ple
    y_ref[0] = _rms(x3, gfin_ref[...])


def _post(x, oTna, oTmla, gates, p, w):
    B, S, D = x.shape
    tm = TM_POST
    weights = (w["w_na_o"], w["w_mla_o"], w["w_out"], w["g_ffn"], w["w_gate"], w["w_up"], w["w_down"],
               w["g_ple"], w["w_ple_gate"], w["w_ple"], w["g_final"])
    in_specs = [
        pl.BlockSpec((1, tm, D), lambda b, i: (b, i, 0)),
        _chunked_spec(NA_WIDTH, tm, NA_GROUP_ROWS * GRID_W),
        _chunked_spec(MLA_WIDTH, tm, MLA_TQ),
        pl.BlockSpec((1, tm, 2 * D), lambda b, i: (b, i, 0)),
        pl.BlockSpec((1, tm, PLE_DIM), lambda b, i: (b, i, 0)),
    ]
    in_specs += [_const_spec(a.shape) for a in weights]
    return pl.pallas_call(
        _post_body,
        grid=(B, S // tm),
        in_specs=in_specs,
        out_specs=pl.BlockSpec((1, tm, D), lambda b, i: (b, i, 0)),
        out_shape=jax.ShapeDtypeStruct((B, S, D), F32),
        compiler_params=pltpu.CompilerParams(
            dimension_semantics=("arbitrary", "arbitrary"), vmem_limit_bytes=VMEM_LIMIT_BYTES),
        name="post",
    )(x, oTna, oTmla, gates, p, *weights)


def _prep_weights(g_mix, w_in, g_q_lat, w_uq, g_kv_lat, w_ukv, w_na_o, w_mla_o, w_out,
                  g_ffn, w_gate_up, w_down, g_ple, w_ple_gate, w_ple, g_final):
    o = np.cumsum((0, NA_WIDTH, NA_WIDTH, NA_WIDTH, Q_LORA, KV_LORA, MLA_ROPE, 2 * D_MODEL))
    w_q, w_k, w_v, w_zq, w_zkv, w_zr, w_g = (w_in[:, o[i]:o[i + 1]] for i in range(7))
    half = MLA_ROPE // 2
    w_zr_rot = jnp.concatenate([-w_zr[:, half:], w_zr[:, :half]], axis=1)

    w_uqT = w_uq.T

    w_ukv_h = w_ukv.reshape(KV_LORA, MLA_HEADS, MLA_NOPE + MLA_V)
    w_uk = w_ukv_h[:, :, :MLA_NOPE].reshape(KV_LORA, MLA_HEADS * MLA_NOPE)
    w_uvT = w_ukv_h[:, :, MLA_NOPE:].reshape(KV_LORA, MLA_WIDTH).T

    w_gate = w_gate_up[:, :D_FF]
    w_up = w_gate_up[:, D_FF:]
    w_dn = w_down

    row = lambda g: g.reshape(1, -1).astype(F32)
    return {
        "g_mix": row(g_mix),
        "w_kna": w_k.astype(BF16),
        "w_qnaT": (w_q * NA_HEAD_DIM ** -0.5).T.astype(BF16),
        "w_vnaT": w_v.T.astype(BF16),
        "w_z": jnp.concatenate([w_zq, w_zkv, w_zr, w_zr_rot], axis=1).astype(BF16),
        "w_g": w_g.astype(BF16),
        "g_q": row(g_q_lat),
        "w_uqT": w_uqT.astype(BF16),
        "g_kv": row(g_kv_lat),
        "w_uk": w_uk.astype(BF16),
        "w_uvT": w_uvT.astype(BF16),
        "w_na_o": w_na_o.astype(BF16),
        "w_mla_o": w_mla_o.astype(BF16),
        "w_out": w_out.astype(BF16),
        "g_ffn": row(g_ffn),
        "w_gate": w_gate.astype(BF16),
        "w_up": w_up.astype(BF16),
        "w_down": w_dn.astype(BF16),
        "g_ple": row(g_ple),
        "w_ple_gate": w_ple_gate.astype(BF16),
        "w_ple": w_ple.astype(BF16),
        "g_final": row(g_final),
    }


def _rope_tables(s_max):
    half = MLA_ROPE // 2
    pos = jnp.arange(s_max, dtype=F32)
    freqs = ROPE_THETA ** (-jnp.arange(half, dtype=F32) / half)
    ang = pos[:, None] * freqs[None, :]
    cos, sin = jnp.cos(ang), jnp.sin(ang)
    return {"cosT": cos.T, "sinT": sin.T,
            "cos": jnp.concatenate([cos, cos], axis=1), "sin": jnp.concatenate([sin, sin], axis=1)}


def _trunk(x, p, w, tabs, biasT):
    kna, qTna, vTna, gates, qT, k, vT = _inproj(x, w, tabs)
    oTna = _na(qTna, kna, vTna, biasT)
    oTmla = _mla(qT, k, vT)
    return _post(x, oTna, oTmla, gates, p, w)


def kernel(x_prompt, x_sample, p_prompt, p_sample, g_mix, w_in, rpb, g_q_lat, w_uq, g_kv_lat, w_ukv,
           w_na_o, w_mla_o, w_out, g_ffn, w_gate_up, w_down, g_ple, w_ple_gate, w_ple, g_final):
    assert g_mix.shape[0] == 1, "single-layer trunk"
    w = _prep_weights(g_mix[0], w_in[0], g_q_lat[0], w_uq[0], g_kv_lat[0], w_ukv[0], w_na_o[0], w_mla_o[0],
                      w_out[0], g_ffn[0], w_gate_up[0], w_down[0], g_ple[0], w_ple_gate[0], w_ple[0], g_final)
    biasT = _na_bias_table(rpb[0])
    tabs = _rope_tables(max(x_prompt.shape[1], x_sample.shape[1]))
    y_prompt = _trunk(x_prompt, p_prompt[0], w, tabs, biasT)
    y_sample = _trunk(x_sample, p_sample[0], w, tabs, biasT)
    return (y_prompt, y_sample)
```

```python
import functools

import numpy as np
import jax
import jax.numpy as jnp
from jax import lax
from jax.experimental import pallas as pl
from jax.experimental.pallas import tpu as pltpu

D_MODEL = 1024
GRID_W = 64
NA_HEADS = 8
NA_HEAD_DIM = 64
NA_WIN_H = 8
NA_WIN_W = 16
MLA_HEADS = 8
MLA_NOPE = 64
MLA_ROPE = 32
MLA_V = 64
Q_LORA = 384
KV_LORA = 256
ROPE_THETA = 10000.0
D_FF = 2816
PLE_DIM = 256
EPS = 1e-6
NA_WIDTH = NA_HEADS * NA_HEAD_DIM
MLA_WIDTH = MLA_HEADS * MLA_V

MLA_QK_PAD = 128
MLA_V_PAD = 80
NA_V_PAD = 80
LOG2_E = 1.4426950408889634
NA_GROUP_ROWS = 4
NA_KEY_ROWS = 12
NA_UNROLL = 32
NA_SCORE_BUFS = 4
NA_LOOKAHEAD = 3
NA_V_CHUNK = 256
MASK_VALUE = -1e30

TM_IN = 512
TM_POST = 512
MLA_TQ = 512
MLA_TK = 256
MLA_UNROLL = 64
MLA_SCORE_BUFS = 4
MLA_LOOKAHEAD = 2
VMEM_LIMIT_BYTES = 56 * 1024 * 1024

BF16 = jnp.bfloat16
F32 = jnp.float32


def _dot(a, b):
    return jnp.dot(a, b, preferred_element_type=F32)


def _dot_nt(a, b):
    return lax.dot_general(a, b, (((1,), (1,)), ((), ())), preferred_element_type=F32)


def _dot_tn(a, b):
    return lax.dot_general(a, b, (((0,), (0,)), ((), ())), preferred_element_type=F32)


def _rms(x, g):
    ms = jnp.mean(x * x, axis=-1, keepdims=True)
    return x * lax.rsqrt(ms + EPS) * g


def _const_spec(shape):
    nd = len(shape)
    return pl.BlockSpec(shape, lambda *_: (0,) * nd, pipeline_mode=pl.Buffered(1))


def _chunked_spec(rows, tm, chunk):
    if tm >= chunk:
        return pl.BlockSpec((1, tm // chunk, rows, chunk), lambda b, i: (b, i, 0, 0))
    sub = chunk // tm
    return pl.BlockSpec((1, 1, rows, tm), lambda b, i: (b, i // sub, 0, i % sub))


def _store_chunked(ref, r0, val):
    width = ref.shape[3]
    for c in range(ref.shape[1]):
        ref[0, c, r0:r0 + val.shape[0], :] = val if val.shape[1] == width else val[:, c * width:(c + 1) * width]


def _inproj_body(x_ref, gmix_ref, wkna_ref, wqnaT_ref, wvnaT_ref, wz_ref, wg_ref,
                 gq_ref, wuqT_ref, gkv_ref, wuk_ref, wuvT_ref,
                 cosT_ref, sinT_ref, cos_ref, sin_ref,
                 kna_ref, qTna_ref, vTna_ref, gates_ref, qT_ref, k_ref, vT_ref):
    tm = x_ref.shape[1]
    h = _rms(x_ref[0], gmix_ref[...]).astype(BF16)

    z = _dot(h, wz_ref[...])
    zqn = _rms(z[:, :Q_LORA], gq_ref[...]).astype(BF16)
    zkvn = _rms(z[:, Q_LORA:Q_LORA + KV_LORA], gkv_ref[...]).astype(BF16)
    zrr = z[:, Q_LORA + KV_LORA:]

    gates_ref[0] = jax.nn.sigmoid(_dot(h, wg_ref[...]))

    assert vT_ref.shape[3] == vTna_ref.shape[3]
    ones_rows = (lax.broadcasted_iota(jnp.int32, (NA_V_PAD - NA_HEAD_DIM, vT_ref.shape[3]), 0) == 0
                 ).astype(F32).astype(BF16)

    scale = (MLA_NOPE + MLA_ROPE) ** -0.5 * LOG2_E
    qT = _dot_nt(wuqT_ref[...], zqn)
    cT = cosT_ref[...]
    sT = sinT_ref[...]
    half = MLA_ROPE // 2
    for hh in range(MLA_HEADS):
        src = hh * (MLA_NOPE + MLA_ROPE)
        base = hh * MLA_QK_PAD
        nope = qT[src:src + MLA_NOPE]
        x1 = qT[src + MLA_NOPE:src + MLA_NOPE + half]
        x2 = qT[src + MLA_NOPE + half:src + MLA_NOPE + MLA_ROPE]
        _store_chunked(qT_ref, base, (nope * scale).astype(BF16))
        _store_chunked(qT_ref, base + MLA_NOPE, ((x1 * cT - x2 * sT) * scale).astype(BF16))
        _store_chunked(qT_ref, base + MLA_NOPE + half, ((x2 * cT + x1 * sT) * scale).astype(BF16))
        _store_chunked(qT_ref, base + MLA_NOPE + MLA_ROPE, jnp.zeros((MLA_QK_PAD - MLA_NOPE - MLA_ROPE, tm), BF16))

    k_rope = zrr[:, :MLA_ROPE] * cos_ref[...] + zrr[:, MLA_ROPE:] * sin_ref[...]
    k_cmp = _dot(zkvn, wuk_ref[...])
    pad = jnp.zeros((tm, MLA_QK_PAD - MLA_NOPE - MLA_ROPE), F32)
    pieces = []
    for hh in range(MLA_HEADS):
        pieces += [k_cmp[:, hh * MLA_NOPE:(hh + 1) * MLA_NOPE], k_rope, pad]
    k_ref[0] = jnp.concatenate(pieces, axis=1).astype(BF16)

    vT = _dot_nt(wuvT_ref[...], zkvn).astype(BF16)
    for hh in range(MLA_HEADS):
        _store_chunked(vT_ref, hh * MLA_V_PAD, vT[hh * MLA_V:(hh + 1) * MLA_V])
        _store_chunked(vT_ref, hh * MLA_V_PAD + MLA_V, ones_rows)

    kna_ref[0] = _dot(h, wkna_ref[...]).astype(BF16)
    _store_chunked(qTna_ref, 0, (_dot_nt(wqnaT_ref[...], h) * LOG2_E).astype(BF16))
    vTna = _dot_nt(wvnaT_ref[...], h).astype(BF16)
    for hh in range(NA_HEADS):
        _store_chunked(vTna_ref, hh * NA_V_PAD, vTna[hh * NA_HEAD_DIM:(hh + 1) * NA_HEAD_DIM])
        _store_chunked(vTna_ref, hh * NA_V_PAD + NA_HEAD_DIM, ones_rows)


def _inproj(x, w, tabs):
    B, S, D = x.shape
    tm = TM_IN
    nt = S // tm
    assert NA_V_PAD - NA_HEAD_DIM == MLA_V_PAD - MLA_V
    hq = MLA_HEADS * MLA_QK_PAD
    na_nq = NA_GROUP_ROWS * GRID_W
    weights = (w["g_mix"], w["w_kna"], w["w_qnaT"], w["w_vnaT"], w["w_z"], w["w_g"],
               w["g_q"], w["w_uqT"], w["g_kv"], w["w_uk"], w["w_uvT"])
    in_specs = [pl.BlockSpec((1, tm, D), lambda b, i: (b, i, 0))]
    in_specs += [_const_spec(a.shape) for a in weights]
    in_specs += [
        pl.BlockSpec((MLA_ROPE // 2, tm), lambda b, i: (0, i)),
        pl.BlockSpec((MLA_ROPE // 2, tm), lambda b, i: (0, i)),
        pl.BlockSpec((tm, MLA_ROPE), lambda b, i: (i, 0)),
        pl.BlockSpec((tm, MLA_ROPE), lambda b, i: (i, 0)),
    ]
    out_shape = (
        jax.ShapeDtypeStruct((B, S, NA_WIDTH), BF16),
        jax.ShapeDtypeStruct((B, S // na_nq, NA_WIDTH, na_nq), BF16),
        jax.ShapeDtypeStruct((B, S // NA_V_CHUNK, NA_HEADS * NA_V_PAD, NA_V_CHUNK), BF16),
        jax.ShapeDtypeStruct((B, S, 2 * D), F32),
        jax.ShapeDtypeStruct((B, S // MLA_TQ, hq, MLA_TQ), BF16),
        jax.ShapeDtypeStruct((B, S, hq), BF16),
        jax.ShapeDtypeStruct((B, S // MLA_TK, MLA_HEADS * MLA_V_PAD, MLA_TK), BF16),
    )
    out_specs = (
        pl.BlockSpec((1, tm, NA_WIDTH), lambda b, i: (b, i, 0)),
        _chunked_spec(NA_WIDTH, tm, na_nq),
        _chunked_spec(NA_HEADS * NA_V_PAD, tm, NA_V_CHUNK),
        pl.BlockSpec((1, tm, 2 * D), lambda b, i: (b, i, 0)),
        _chunked_spec(hq, tm, MLA_TQ),
        pl.BlockSpec((1, tm, hq), lambda b, i: (b, i, 0)),
        _chunked_spec(MLA_HEADS * MLA_V_PAD, tm, MLA_TK),
    )
    return pl.pallas_call(
        _inproj_body,
        grid=(B, nt),
        in_specs=in_specs,
        out_specs=out_specs,
        out_shape=out_shape,
        compiler_params=pltpu.CompilerParams(
            dimension_semantics=("arbitrary", "arbitrary"), vmem_limit_bytes=VMEM_LIMIT_BYTES),
        name="inproj",
    )(x, *weights, tabs["cosT"][:, :S], tabs["sinT"][:, :S], tabs["cos"][:S], tabs["sin"][:S])


def _na_body(qT_ref, k_ref, vT_ref, bias_ref, o_ref, *s_refs, rows, unroll):
    ng, nq = qT_ref.shape[1], qT_ref.shape[3]
    nk = NA_KEY_ROWS * GRID_W
    nbuf = len(s_refs)
    row = lax.broadcasted_iota(jnp.int32, (2 * NA_HEAD_DIM, nq), 0)

    def key_row0(g):
        return jnp.clip(NA_GROUP_ROWS * g - NA_WIN_H // 2, 0, rows - NA_KEY_ROWS)

    def scores_into(g, hh, s_ref):
        tok0 = pl.multiple_of(key_row0(g) * GRID_W, NA_V_CHUNK)
        kwin = k_ref[0, pl.ds(tok0, nk), :]
        qT = qT_ref[0, g]
        lo = hh * NA_HEAD_DIM
        q_h = jnp.where((row >= lo) & (row < lo + NA_HEAD_DIM), qT, jnp.zeros_like(qT))
        kind = jnp.where(g == 0, 0, jnp.where(g == ng - 1, 2, 1))
        s = _dot(kwin, q_h) + bias_ref[kind, hh]
        s_ref[...] = s
        return jnp.max(s, axis=0, keepdims=True)

    def softmax_pv(g, hh, s_ref, mx):
        c0 = lax.shift_right_logical(key_row0(g), 2)
        pb = jnp.exp2(s_ref[...] - mx).astype(BF16)
        acc = jnp.zeros((NA_V_PAD, nq), F32)
        for c in range(nk // NA_V_CHUNK):
            v_c = vT_ref[0, c0 + c, hh * NA_V_PAD:(hh + 1) * NA_V_PAD, :]
            acc = acc + _dot(v_c, pb[c * NA_V_CHUNK:(c + 1) * NA_V_CHUNK])
        o_ref[0, g, hh * NA_HEAD_DIM:(hh + 1) * NA_HEAD_DIM, :] = (
            acc[:NA_HEAD_DIM] / acc[NA_HEAD_DIM:NA_HEAD_DIM + 1]).astype(BF16)

    def trip(i, mxs):
        g0 = i * (unroll // 2)
        for u in range(unroll):
            ahead = u + NA_LOOKAHEAD
            g_ahead = jnp.minimum(g0 + ahead // 2, ng - 1)
            mx_new = scores_into(g_ahead, ahead % 2, s_refs[ahead % nbuf])
            softmax_pv(g0 + u // 2, u % 2, s_refs[u % nbuf], mxs[0])
            mxs = mxs[1:] + (mx_new,)
        return mxs

    mxs = tuple(scores_into(t // 2, t % 2, s_refs[t]) for t in range(NA_LOOKAHEAD))
    lax.fori_loop(0, 2 * ng // unroll, trip, mxs)


def _na(qT, k, vT, biasT):
    B, ng, _, nq = qT.shape
    S = k.shape[1]
    rows = S // GRID_W
    nk = NA_KEY_ROWS * GRID_W
    pair = 2 * NA_HEAD_DIM
    assert nq == NA_GROUP_ROWS * GRID_W and rows >= NA_KEY_ROWS
    unroll = next(u for u in range(NA_UNROLL, 0, -NA_SCORE_BUFS) if (2 * ng) % u == 0)
    assert NA_UNROLL % NA_SCORE_BUFS == 0 and NA_SCORE_BUFS % 2 == 0 and NA_LOOKAHEAD < NA_SCORE_BUFS <= unroll
    return pl.pallas_call(
        functools.partial(_na_body, rows=rows, unroll=unroll),
        grid=(B, NA_HEADS // 2),
        in_specs=[
            pl.BlockSpec((1, ng, pair, nq), lambda b, hp: (b, 0, hp, 0)),
            pl.BlockSpec((1, S, pair), lambda b, hp: (b, 0, hp)),
            pl.BlockSpec((1, S // NA_V_CHUNK, 2 * NA_V_PAD, NA_V_CHUNK), lambda b, hp: (b, 0, hp, 0)),
            pl.BlockSpec((3, 2, nk, nq), lambda b, hp: (0, hp, 0, 0)),
        ],
        out_specs=pl.BlockSpec((1, ng, pair, nq), lambda b, hp: (b, 0, hp, 0)),
        out_shape=jax.ShapeDtypeStruct((B, ng, NA_WIDTH, nq), BF16),
        scratch_shapes=[pltpu.VMEM((nk, nq), F32)] * NA_SCORE_BUFS,
        compiler_params=pltpu.CompilerParams(
            dimension_semantics=("arbitrary", "arbitrary"), vmem_limit_bytes=VMEM_LIMIT_BYTES),
        name="natten",
    )(qT, k, vT, biasT)


def _na_bias_table(rpb):
    n_off_r, n_off_c = 2 * NA_WIN_H - 1, 2 * NA_WIN_W - 1
    cols = np.arange(GRID_W)
    cs = np.clip(cols - NA_WIN_W // 2, 0, GRID_W - NA_WIN_W)
    col_ok = (cols[:, None] >= cs[None, :]) & (cols[:, None] < cs[None, :] + NA_WIN_W)
    col_idx = np.clip(cols[:, None] - cols[None, :] + NA_WIN_W - 1, 0, n_off_c - 1)
    onehot = (col_idx[None] == np.arange(n_off_c)[:, None, None]) & col_ok[None]
    onehot = jnp.asarray(onehot.reshape(n_off_c, GRID_W * GRID_W), F32)
    blocks = jnp.dot(rpb.reshape(NA_HEADS * n_off_r, n_off_c), onehot, precision=lax.Precision.HIGHEST)
    blocks = blocks.reshape(NA_HEADS, n_off_r, GRID_W, GRID_W)
    blocks = jnp.where(jnp.asarray(col_ok)[None, None], blocks * LOG2_E, MASK_VALUE)
    masked = jnp.full((NA_HEADS, 1, GRID_W, GRID_W), MASK_VALUE, F32)
    blocks = jnp.concatenate([blocks, masked], axis=1)

    rows = 4 * NA_KEY_ROWS
    kinds = ((0, 0), (NA_KEY_ROWS, NA_KEY_ROWS - NA_WIN_H // 2), (rows - NA_GROUP_ROWS, rows - NA_KEY_ROWS))
    sel = np.zeros((3, NA_KEY_ROWS, NA_GROUP_ROWS), np.int32)
    for t, (r0, ws) in enumerate(kinds):
        q_row = r0 + np.arange(NA_GROUP_ROWS)
        k_row = ws + np.arange(NA_KEY_ROWS)
        rs = np.clip(q_row - NA_WIN_H // 2, 0, rows - NA_WIN_H)
        row_ok = (k_row[:, None] >= rs[None, :]) & (k_row[:, None] < rs[None, :] + NA_WIN_H)
        sel[t] = np.where(row_ok, k_row[:, None] - q_row[None, :] + NA_WIN_H - 1, n_off_r)
    table = jnp.take(blocks, jnp.asarray(sel.reshape(-1)), axis=1)
    table = table.reshape(NA_HEADS, 3, NA_KEY_ROWS, NA_GROUP_ROWS, GRID_W, GRID_W)
    table = table.transpose(1, 0, 2, 4, 3, 5)
    return table.reshape(3, NA_HEADS, NA_KEY_ROWS * GRID_W, NA_GROUP_ROWS * GRID_W)


def _mla_body(qT_ref, k_ref, vT_ref, o_ref, *s_refs, unroll):
    nq, tq = qT_ref.shape[1], qT_ref.shape[3]
    nchunk, tk = vT_ref.shape[1], vT_ref.shape[3]
    nbuf = len(s_refs)

    def scores_into(qb, j, s_ref):
        s = _dot(k_ref[0, j * tk:(j + 1) * tk, :], qT_ref[0, qb])
        s_ref[...] = s
        return jnp.max(s, axis=0, keepdims=True)

    def softmax_pv(j, s_ref, mx, m, acc):
        m_new = jnp.maximum(m, mx)
        alpha = jnp.exp2(m - m_new)
        p = jnp.exp2(s_ref[...] - m_new).astype(BF16)
        return m_new, alpha * acc + _dot(vT_ref[0, j], p)

    blocks_per_trip = unroll // nchunk

    def trip(i, mxs):
        qb0 = i * blocks_per_trip
        qb1 = jnp.minimum(qb0 + blocks_per_trip, nq - blocks_per_trip)
        for u in range(unroll):
            if u % nchunk == 0:
                m, acc = jnp.full((1, tq), -jnp.inf, F32), jnp.zeros((MLA_V_PAD, tq), F32)
            ahead = u + MLA_LOOKAHEAD
            qb_ahead = (qb0 if ahead < unroll else qb1) + (ahead % unroll) // nchunk
            mx_new = scores_into(qb_ahead, ahead % nchunk, s_refs[ahead % nbuf])
            m, acc = softmax_pv(u % nchunk, s_refs[u % nbuf], mxs[0], m, acc)
            mxs = mxs[1:] + (mx_new,)
            if (u + 1) % nchunk == 0:
                o_ref[0, qb0 + u // nchunk] = (acc[:MLA_V] / acc[MLA_V:MLA_V + 1]).astype(BF16)
        return mxs

    mxs = tuple(scores_into(0, t, s_refs[t]) for t in range(MLA_LOOKAHEAD))
    lax.fori_loop(0, nq // blocks_per_trip, trip, mxs)


def _mla(qT, k, vT):
    B, nq, _, tq = qT.shape
    S = k.shape[1]
    nchunk, tk = vT.shape[1], vT.shape[3]
    unroll = next(u for u in range(MLA_UNROLL, 0, -MLA_SCORE_BUFS)
                  if u % nchunk == 0 and nq % (u // nchunk) == 0)
    assert MLA_UNROLL % MLA_SCORE_BUFS == 0 and MLA_LOOKAHEAD < MLA_SCORE_BUFS <= unroll
    return pl.pallas_call(
        functools.partial(_mla_body, unroll=unroll),
        grid=(B, MLA_HEADS),
        in_specs=[
            pl.BlockSpec((1, nq, MLA_QK_PAD, tq), lambda b, h: (b, 0, h, 0)),
            pl.BlockSpec((1, S, MLA_QK_PAD), lambda b, h: (b, 0, h)),
            pl.BlockSpec((1, nchunk, MLA_V_PAD, tk), lambda b, h: (b, 0, h, 0)),
        ],
        out_specs=pl.BlockSpec((1, nq, MLA_V, tq), lambda b, h: (b, 0, h, 0)),
        out_shape=jax.ShapeDtypeStruct((B, nq, MLA_WIDTH, tq), BF16),
        scratch_shapes=[pltpu.VMEM((tk, tq), F32)] * MLA_SCORE_BUFS,
        compiler_params=pltpu.CompilerParams(
            dimension_semantics=("arbitrary", "arbitrary"), vmem_limit_bytes=VMEM_LIMIT_BYTES),
        name="mla",
    )(qT, k, vT)


def _post_body(x_ref, oTna_ref, oTmla_ref, gates_ref, p_ref,
               wnao_ref, wmlao_ref, wout_ref, gffn_ref, wgu_ref, wdown_ref,
               gple_ref, wpleg_ref, wple_ref, gfin_ref, y_ref):
    d = x_ref.shape[2]
    oTna = jnp.concatenate([oTna_ref[0, c] for c in range(oTna_ref.shape[1])], axis=1)
    a = _dot_tn(oTna, wnao_ref[...])
    oTmla = jnp.concatenate([oTmla_ref[0, c] for c in range(oTmla_ref.shape[1])], axis=1)
    b = _dot_tn(oTmla, wmlao_ref[...])
    gates = gates_ref[0]
    mix = (gates[:, :d] * a + gates[:, d:] * b).astype(BF16)
    x1 = x_ref[0] + _dot(mix, wout_ref[...])
    ple = _dot(p_ref[0].astype(BF16), wple_ref[...])

    hn = _rms(x1, gffn_ref[...]).astype(BF16)
    gu = _dot(hn, wgu_ref[...])
    gate, up = gu[:, :D_FF], gu[:, D_FF:]
    act = (gate * jax.nn.sigmoid(gate) * up).astype(BF16)
    x2 = x1 + _dot(act, wdown_ref[...])

    ple_gate = jax.nn.sigmoid(_dot(_rms(x2, gple_ref[...]).astype(BF16), wpleg_ref[...]))
    x3 = x2 + ple_gate * ple
    y_ref[0] = _rms(x3, gfin_ref[...])


def _post(x, oTna, oTmla, gates, p, w):
    B, S, D = x.shape
    tm = TM_POST
    weights = (w["w_na_o"], w["w_mla_o"], w["w_out"], w["g_ffn"], w["w_gate_up"], w["w_down"],
               w["g_ple"], w["w_ple_gate"], w["w_ple"], w["g_final"])
    in_specs = [
        pl.BlockSpec((1, tm, D), lambda b, i: (b, i, 0)),
        _chunked_spec(NA_WIDTH, tm, NA_GROUP_ROWS * GRID_W),
        _chunked_spec(MLA_WIDTH, tm, MLA_TQ),
        pl.BlockSpec((1, tm, 2 * D), lambda b, i: (b, i, 0)),
        pl.BlockSpec((1, tm, PLE_DIM), lambda b, i: (b, i, 0)),
    ]
    in_specs += [_const_spec(a.shape) for a in weights]
    return pl.pallas_call(
        _post_body,
        grid=(B, S // tm),
        in_specs=in_specs,
        out_specs=pl.BlockSpec((1, tm, D), lambda b, i: (b, i, 0)),
        out_shape=jax.ShapeDtypeStruct((B, S, D), F32),
        compiler_params=pltpu.CompilerParams(
            dimension_semantics=("arbitrary", "arbitrary"), vmem_limit_bytes=VMEM_LIMIT_BYTES),
        name="post",
    )(x, oTna, oTmla, gates, p, *weights)


def _prep_weights(g_mix, w_in, g_q_lat, w_uq, g_kv_lat, w_ukv, w_na_o, w_mla_o, w_out,
                  g_ffn, w_gate_up, w_down, g_ple, w_ple_gate, w_ple, g_final):
    o = np.cumsum((0, NA_WIDTH, NA_WIDTH, NA_WIDTH, Q_LORA, KV_LORA, MLA_ROPE, 2 * D_MODEL))
    w_q, w_k, w_v, w_zq, w_zkv, w_zr, w_g = (w_in[:, o[i]:o[i + 1]] for i in range(7))
    half = MLA_ROPE // 2
    w_zr_rot = jnp.concatenate([-w_zr[:, half:], w_zr[:, :half]], axis=1)

    w_uqT = w_uq.T

    w_ukv_h = w_ukv.reshape(KV_LORA, MLA_HEADS, MLA_NOPE + MLA_V)
    w_uk = w_ukv_h[:, :, :MLA_NOPE].reshape(KV_LORA, MLA_HEADS * MLA_NOPE)
    w_uvT = w_ukv_h[:, :, MLA_NOPE:].reshape(KV_LORA, MLA_WIDTH).T

    w_dn = w_down

    row = lambda g: g.reshape(1, -1).astype(F32)
    return {
        "g_mix": row(g_mix),
        "w_kna": w_k.astype(BF16),
        "w_qnaT": (w_q * NA_HEAD_DIM ** -0.5).T.astype(BF16),
        "w_vnaT": w_v.T.astype(BF16),
        "w_z": jnp.concatenate([w_zq, w_zkv, w_zr, w_zr_rot], axis=1).astype(BF16),
        "w_g": w_g.astype(BF16),
        "g_q": row(g_q_lat),
        "w_uqT": w_uqT.astype(BF16),
        "g_kv": row(g_kv_lat),
        "w_uk": w_uk.astype(BF16),
        "w_uvT": w_uvT.astype(BF16),
        "w_na_o": w_na_o.astype(BF16),
        "w_mla_o": w_mla_o.astype(BF16),
        "w_out": w_out.astype(BF16),
        "g_ffn": row(g_ffn),
        "w_gate_up": w_gate_up.astype(BF16),
        "w_down": w_dn.astype(BF16),
        "g_ple": row(g_ple),
        "w_ple_gate": w_ple_gate.astype(BF16),
        "w_ple": w_ple.astype(BF16),
        "g_final": row(g_final),
    }


def _rope_tables(s_max):
    half = MLA_ROPE // 2
    pos = jnp.arange(s_max, dtype=F32)
    freqs = ROPE_THETA ** (-jnp.arange(half, dtype=F32) / half)
    ang = pos[:, None] * freqs[None, :]
    cos, sin = jnp.cos(ang), jnp.sin(ang)
    return {"cosT": cos.T, "sinT": sin.T,
            "cos": jnp.concatenate([cos, cos], axis=1), "sin": jnp.concatenate([sin, sin], axis=1)}


def _trunk(x, p, w, tabs, biasT):
    kna, qTna, vTna, gates, qT, k, vT = _inproj(x, w, tabs)
    oTna = _na(qTna, kna, vTna, biasT)
    oTmla = _mla(qT, k, vT)
    return _post(x, oTna, oTmla, gates, p, w)


def kernel(x_prompt, x_sample, p_prompt, p_sample, g_mix, w_in, rpb, g_q_lat, w_uq, g_kv_lat, w_ukv,
           w_na_o, w_mla_o, w_out, g_ffn, w_gate_up, w_down, g_ple, w_ple_gate, w_ple, g_final):
    assert g_mix.shape[0] == 1, "single-layer trunk"
    w = _prep_weights(g_mix[0], w_in[0], g_q_lat[0], w_uq[0], g_kv_lat[0], w_ukv[0], w_na_o[0], w_mla_o[0],
                      w_out[0], g_ffn[0], w_gate_up[0], w_down[0], g_ple[0], w_ple_gate[0], w_ple[0], g_final)
    biasT = _na_bias_table(rpb[0])
    tabs = _rope_tables(max(x_prompt.shape[1], x_sample.shape[1]))
    y_prompt = _trunk(x_prompt, p_prompt[0], w, tabs, biasT)
    y_sample = _trunk(x_sample, p_sample[0], w, tabs, biasT)
    return (y_prompt, y_sample)
```
